```python
import math
import jax, jax.numpy as jnp
from jax import lax
import numpy as np

D_MODEL = 2048
BATCH = 16
SEQ = 2048
DEPTH = 2
DEC_BATCH = 4
DEC_SEQ = 2048
PAST_LEN = 128

N_META = 16
Q_BLOCK = 128
ROPE_THETA = 10000.0
EPS = 1e-6

DIFF_WIDTH = D_MODEL // 2
DIFF_HEAD_DIM = 64
DIFF_HEADS = DIFF_WIDTH // (2 * DIFF_HEAD_DIM)
DIFF_V_DIM = 2 * DIFF_HEAD_DIM
CONV_WIDTH = D_MODEL - DIFF_WIDTH
CONV_K = 3
EVEN_SPLITS = [DIFF_WIDTH, DIFF_WIDTH, DIFF_WIDTH, CONV_WIDTH, CONV_WIDTH, CONV_WIDTH]
EVEN_IN = sum(EVEN_SPLITS)

MLA_HEADS = 16
MLA_NOPE = 128
MLA_ROPE = 64
MLA_V = 128
MLA_Q_RANK = 512
MLA_KV_RANK = 512
ODD_IN = MLA_Q_RANK + MLA_KV_RANK + MLA_ROPE

N_EXPERTS = 16
EXPERT_FF = 1024
EC_CAPACITY_FACTOR = 2

N_EVEN = (DEPTH + 1) // 2
N_ODD = DEPTH // 2

kernel_name = 'hybrid_diffattn_shortconv_mla_ec_encoder'


def rms_norm(x, g):
    xf = x.astype(jnp.float32)
    y = xf * lax.rsqrt(jnp.mean(xf * xf, axis=-1, keepdims=True) + EPS)
    return (y * g.astype(jnp.float32)).astype(x.dtype)


def rope_tables(length, dim, dtype):
    inv = 1.0 / (ROPE_THETA ** (jnp.arange(0, dim, 2, dtype=jnp.float32) / dim))
    ang = jnp.arange(length, dtype=jnp.float32)[:, None] * inv[None, :]
    return jnp.cos(ang).astype(dtype), jnp.sin(ang).astype(dtype)


def apply_rope(x, cos, sin):
    shape = (1, cos.shape[0]) + (1,) * (x.ndim - 3) + (cos.shape[1],)
    c = cos.reshape(shape)
    s = sin.reshape(shape)
    x1, x2 = jnp.split(x, 2, axis=-1)
    return jnp.concatenate([x1 * c - x2 * s, x1 * s + x2 * c], axis=-1)


def sweep_query_blocks(block_fn, *qs):
    meta_out = block_fn(*[q[:, :N_META] for q in qs])
    B, L = qs[0].shape[:2]
    nblk = (L - N_META) // Q_BLOCK
    blocks = tuple(jnp.moveaxis(q[:, N_META:].reshape((B, nblk, Q_BLOCK) + q.shape[2:]), 1, 0) for q in qs)
    out = lax.map(lambda bs: block_fn(*bs), blocks)
    out = jnp.moveaxis(out, 0, 1).reshape((B, L - N_META) + out.shape[3:])
    return jnp.concatenate([meta_out, out], axis=1)


def even_mixer(h, w_in, w_out, lq1, lk1, lq2, lk2, subln_g, conv_w, lambda_init, cos, sin):
    B, L, _ = h.shape
    proj = h @ w_in
    q, k, v, gate_b, gate_c, cx = jnp.split(proj, list(np.cumsum(EVEN_SPLITS)[:-1]), axis=-1)
    q = apply_rope(q.reshape(B, L, DIFF_HEADS, 2, DIFF_HEAD_DIM), cos, sin)
    k = apply_rope(k.reshape(B, L, DIFF_HEADS, 2, DIFF_HEAD_DIM), cos, sin)
    v = v.reshape(B, L, DIFF_HEADS, DIFF_V_DIM)
    lam = (jnp.exp(jnp.sum(lq1.astype(jnp.float32) * lk1.astype(jnp.float32)))
           - jnp.exp(jnp.sum(lq2.astype(jnp.float32) * lk2.astype(jnp.float32))) + lambda_init)
    scale = DIFF_HEAD_DIM ** -0.5

    def block(qb):
        s = jnp.einsum('bqhmd,bkhmd->bmhqk', qb, k, preferred_element_type=jnp.float32) * scale
        p = jax.nn.softmax(s, axis=-1)
        a = p[:, 0] - lam * p[:, 1]
        return jnp.einsum('bhqk,bkhd->bqhd', a.astype(v.dtype), v)

    o = sweep_query_blocks(block, q)
    attn_out = (rms_norm(o, subln_g) * (1.0 - lambda_init)).reshape(B, L, DIFF_WIDTH)
    u = gate_c * cx
    up = jnp.pad(u, ((0, 0), (1, 1), (0, 0)))
    y = up[:, :-2] * conv_w[0] + up[:, 1:-1] * conv_w[1] + up[:, 2:] * conv_w[2]
    conv_out = gate_b * y
    return jnp.concatenate([attn_out, conv_out], axis=-1) @ w_out


def odd_mixer(h, w_in, q_norm_g, w_q_b, kv_norm_g, w_kv_b, w_out, cos, sin):
    B, L, _ = h.shape
    proj = h @ w_in
    cq, ckv, k_rope = jnp.split(proj, [MLA_Q_RANK, MLA_Q_RANK + MLA_KV_RANK], axis=-1)
    q = (rms_norm(cq, q_norm_g) @ w_q_b).reshape(B, L, MLA_HEADS, MLA_NOPE + MLA_ROPE)
    q_nope = q[..., :MLA_NOPE]
    q_rope = apply_rope(q[..., MLA_NOPE:], cos, sin)
    k_rope = apply_rope(k_rope, cos, sin)
    kv = (rms_norm(ckv, kv_norm_g) @ w_kv_b).reshape(B, L, MLA_HEADS, MLA_NOPE + MLA_V)
    k_nope = kv[..., :MLA_NOPE]
    v = kv[..., MLA_NOPE:]
    scale = (MLA_NOPE + MLA_ROPE) ** -0.5

    def block(qn, qr):
        s = (jnp.einsum('bqhd,bkhd->bhqk', qn, k_nope, preferred_element_type=jnp.float32)
             + jnp.einsum('bqhr,bkr->bhqk', qr, k_rope, preferred_element_type=jnp.float32)) * scale
        p = jax.nn.softmax(s, axis=-1)
        return jnp.einsum('bhqk,bkhd->bqhd', p.astype(v.dtype), v)

    o = sweep_query_blocks(block, q_nope, q_rope)
    return o.reshape(B, L, MLA_HEADS * MLA_V) @ w_out


def expert_choice_ffn(h, w_router, w_gate, w_up, w_down):
    B, L, D = h.shape
    n = B * L
    cap = EC_CAPACITY_FACTOR * n // N_EXPERTS
    t = h.reshape(n, D)
    aff = jax.nn.softmax((t @ w_router).astype(jnp.float32), axis=-1)
    g, idx = lax.top_k(aff.T, cap)
    xs = t[idx]
    hid = jax.nn.silu(jnp.einsum('ecd,edf->ecf', xs, w_gate)) * jnp.einsum('ecd,edf->ecf', xs, w_up)
    ye = jnp.einsum('ecf,efd->ecd', hid, w_down) * g[..., None].astype(t.dtype)
    out = jnp.zeros_like(t).at[idx.reshape(-1)].add(ye.reshape(-1, D))
    return out.reshape(B, L, D)


def lambda_init_fn(layer):
    return 0.8 - 0.6 * math.exp(-0.3 * layer)


def trunk(x, p):
    B = x.shape[0]
    meta = jnp.broadcast_to(p['meta_tokens'][None].astype(x.dtype), (B, N_META, D_MODEL))
    h = jnp.concatenate([meta, x], axis=1)
    L = h.shape[1]
    cos, sin = rope_tables(L, DIFF_HEAD_DIM, h.dtype)
    for layer in range(DEPTH):
        hn = rms_norm(h, p['mix_norm_g'][layer])
        if layer % 2 == 0:
            i = layer // 2
            h = h + even_mixer(hn, p['ev_w_in'][i], p['ev_w_out'][i], p['ev_lambda_q1'][i], p['ev_lambda_k1'][i],
                               p['ev_lambda_q2'][i], p['ev_lambda_k2'][i], p['ev_subln_g'][i], p['ev_conv_w'][i],
                               lambda_init_fn(layer), cos, sin)
        else:
            i = layer // 2
            h = h + odd_mixer(hn, p['od_w_in'][i], p['od_q_norm_g'][i], p['od_w_q_b'][i], p['od_kv_norm_g'][i],
                              p['od_w_kv_b'][i], p['od_w_out'][i], cos, sin)
        h = h + expert_choice_ffn(rms_norm(h, p['ffn_norm_g'][layer]), p['moe_w_router'][layer],
                                  p['moe_w_gate'][layer], p['moe_w_up'][layer], p['moe_w_down'][layer])
    h = rms_norm(h, p['final_norm_g'])
    return h[:, N_META:]


def _normal(k, shape, scale):
    return jax.random.normal(k, shape, jnp.float32) * scale


def _gain(k, shape):
    return 1.0 + 0.02 * jax.random.normal(k, shape, jnp.float32)


def setup_inputs(seed: int = 0) -> dict:
    key = jax.random.key(seed)
    ks = jax.random.split(key, 26)
    D = D_MODEL
    return {
        'x_prompt': _normal(ks[0], (BATCH, SEQ, D), 1.0),
        'x_sample': _normal(ks[1], (DEC_BATCH, DEC_SEQ, D), 1.0),
        'meta_tokens': _normal(ks[2], (N_META, D), 1.0),
        'mix_norm_g': _gain(ks[3], (DEPTH, D)),
        'ffn_norm_g': _gain(ks[4], (DEPTH, D)),
        'final_norm_g': _gain(ks[5], (D,)),
        'ev_w_in': _normal(ks[6], (N_EVEN, D, EVEN_IN), D ** -0.5),
        'ev_w_out': _normal(ks[7], (N_EVEN, D, D), D ** -0.5),
        'ev_lambda_q1': _normal(ks[8], (N_EVEN, DIFF_HEAD_DIM), 0.1),
        'ev_lambda_k1': _normal(ks[9], (N_EVEN, DIFF_HEAD_DIM), 0.1),
        'ev_lambda_q2': _normal(ks[10], (N_EVEN, DIFF_HEAD_DIM), 0.1),
        'ev_lambda_k2': _normal(ks[11], (N_EVEN, DIFF_HEAD_DIM), 0.1),
        'ev_subln_g': _gain(ks[12], (N_EVEN, DIFF_V_DIM)),
        'ev_conv_w': _normal(ks[13], (N_EVEN, CONV_K, CONV_WIDTH), CONV_K ** -0.5),
        'od_w_in': _normal(ks[14], (N_ODD, D, ODD_IN), D ** -0.5),
        'od_q_norm_g': _gain(ks[15], (N_ODD, MLA_Q_RANK)),
        'od_w_q_b': _normal(ks[16], (N_ODD, MLA_Q_RANK, MLA_HEADS * (MLA_NOPE + MLA_ROPE)), MLA_Q_RANK ** -0.5),
        'od_kv_norm_g': _gain(ks[17], (N_ODD, MLA_KV_RANK)),
        'od_w_kv_b': _normal(ks[18], (N_ODD, MLA_KV_RANK, MLA_HEADS * (MLA_NOPE + MLA_V)), MLA_KV_RANK ** -0.5),
        'od_w_out': _normal(ks[19], (N_ODD, MLA_HEADS * MLA_V, D), (MLA_HEADS * MLA_V) ** -0.5),
        'moe_w_router': _normal(ks[20], (DEPTH, D, N_EXPERTS), D ** -0.5),
        'moe_w_gate': _normal(ks[21], (DEPTH, N_EXPERTS, D, EXPERT_FF), D ** -0.5),
        'moe_w_up': _normal(ks[22], (DEPTH, N_EXPERTS, D, EXPERT_FF), D ** -0.5),
        'moe_w_down': _normal(ks[23], (DEPTH, N_EXPERTS, EXPERT_FF, D), EXPERT_FF ** -0.5),
    }


def reference(x_prompt, x_sample, meta_tokens, mix_norm_g, ffn_norm_g, final_norm_g,
              ev_w_in, ev_w_out, ev_lambda_q1, ev_lambda_k1, ev_lambda_q2, ev_lambda_k2, ev_subln_g, ev_conv_w,
              od_w_in, od_q_norm_g, od_w_q_b, od_kv_norm_g, od_w_kv_b, od_w_out,
              moe_w_router, moe_w_gate, moe_w_up, moe_w_down):
    p = {
        'meta_tokens': meta_tokens, 'mix_norm_g': mix_norm_g, 'ffn_norm_g': ffn_norm_g,
        'final_norm_g': final_norm_g, 'ev_w_in': ev_w_in, 'ev_w_out': ev_w_out,
        'ev_lambda_q1': ev_lambda_q1, 'ev_lambda_k1': ev_lambda_k1, 'ev_lambda_q2': ev_lambda_q2,
        'ev_lambda_k2': ev_lambda_k2, 'ev_subln_g': ev_subln_g, 'ev_conv_w': ev_conv_w,
        'od_w_in': od_w_in, 'od_q_norm_g': od_q_norm_g, 'od_w_q_b': od_w_q_b,
        'od_kv_norm_g': od_kv_norm_g, 'od_w_kv_b': od_w_kv_b, 'od_w_out': od_w_out,
        'moe_w_router': moe_w_router, 'moe_w_gate': moe_w_gate, 'moe_w_up': moe_w_up,
        'moe_w_down': moe_w_down,
    }
    y_prompt = trunk(x_prompt, p)
    y_sample = trunk(x_sample, p)
    return (y_prompt, y_sample)
```

```python
import functools

import jax
import jax.numpy as jnp
from jax import lax
from jax.experimental import pallas as pl
from jax.experimental.pallas import tpu as pltpu

N_META = 16
EPS = 1e-6
ROPE_THETA = 10000.0
HEAD_DIM = 64
LANES = 128
MLA_NOPE = 128
MLA_ROPE = 64
MLA_V = 128
MLA_QK_PAD = 256
N_EXPERT_CAPACITY_FACTOR = 2
VMEM_LIMIT_BYTES = 56 * 1024 * 1024
NEG_BIG = -1e30
PAIR_ROW_SHIFT = 12
PAIR_PLANE_SHIFT = 8
MAX_PLANES = 16

F32 = jnp.float32
BF16 = jnp.bfloat16


def _tile(n, cap, mult):
    best = None
    for t in range(mult, min(n, cap) + 1, mult):
        if n % t == 0:
            best = t
    assert best is not None, (n, cap, mult)
    return best


def _params(*sem):
    return pltpu.CompilerParams(dimension_semantics=sem, vmem_limit_bytes=VMEM_LIMIT_BYTES)


def _rms(x, g):
    return x * lax.rsqrt(jnp.mean(x * x, axis=-1, keepdims=True) + EPS) * g


def _rope128(x, cos, sin_signed):
    lane = lax.broadcasted_iota(jnp.int32, x.shape, 1)
    first_half = (lane % HEAD_DIM) < (HEAD_DIM // 2)
    rot = jnp.where(first_half, pltpu.roll(x, LANES - HEAD_DIM // 2, 1), pltpu.roll(x, HEAD_DIM // 2, 1))
    return x * cos + rot * sin_signed


def _rope_tables(length):
    inv = 1.0 / (ROPE_THETA ** (jnp.arange(0, HEAD_DIM, 2, dtype=F32) / HEAD_DIM))
    ang = jnp.arange(length, dtype=F32)[:, None] * inv[None, :]
    cos, sin = jnp.cos(ang), jnp.sin(ang)
    cos128 = jnp.tile(cos, (1, LANES // (HEAD_DIM // 2)))
    sin128 = jnp.tile(jnp.concatenate([-sin, sin], axis=1), (1, LANES // HEAD_DIM))
    return cos128, sin128


def _even_proj_kernel(x_ref, g_ref, w_ref, cos_ref, sin_ref, o_ref, xn_ref, *, q_scale):
    j = pl.program_id(1)

    @pl.when(j == 0)
    def _():
        xn_ref[...] = _rms(x_ref[...], g_ref[...]).astype(BF16)

    acc = jnp.dot(xn_ref[...], w_ref[...], preferred_element_type=F32)

    @pl.when(j < 2)
    def _():
        scale = jnp.where(j == 0, q_scale, 1.0).astype(F32)
        cos, sin = cos_ref[...], sin_ref[...]
        for c in range(acc.shape[1] // LANES):
            sl = slice(c * LANES, (c + 1) * LANES)
            o_ref[:, sl] = (_rope128(acc[:, sl], cos, sin) * scale).astype(o_ref.dtype)

    @pl.when(j >= 2)
    def _():
        o_ref[...] = acc.astype(o_ref.dtype)


def _even_proj(h, g, w_in, cos128, sin128, seq):
    nt, d = h.shape
    n_out = w_in.shape[1]
    tn = n_out // 6
    tm = _tile(seq, 768, 16)
    tiles_per_seq = seq // tm
    return pl.pallas_call(
        functools.partial(_even_proj_kernel, q_scale=HEAD_DIM ** -0.5),
        grid=(nt // tm, 6),
        in_specs=[
            pl.BlockSpec((tm, d), lambda i, j: (i, 0)),
            pl.BlockSpec((1, d), lambda i, j: (0, 0)),
            pl.BlockSpec((d, tn), lambda i, j: (0, j)),
            pl.BlockSpec((tm, LANES), lambda i, j: (i % tiles_per_seq, 0)),
            pl.BlockSpec((tm, LANES), lambda i, j: (i % tiles_per_seq, 0)),
        ],
        out_specs=pl.BlockSpec((tm, tn), lambda i, j: (i, j)),
        out_shape=jax.ShapeDtypeStruct((nt, n_out), BF16),
        scratch_shapes=[pltpu.VMEM((tm, d), BF16)],
        compiler_params=_params("parallel", "arbitrary"),
        name="even_proj",
    )(h, g, w_in, cos128, sin128)


def _softmax_pv(s, tail_bias, v_ref):
    lp = s.shape[1]
    s_main, s_tail = s[:, : lp - LANES], s[:, lp - LANES:] + tail_bias
    m = jnp.maximum(jnp.max(s_main, axis=-1, keepdims=True), jnp.max(s_tail, axis=-1, keepdims=True))
    p_main, p_tail = jnp.exp(s_main - m), jnp.exp(s_tail - m)
    l = jnp.sum(p_main, axis=-1, keepdims=True) + jnp.sum(p_tail, axis=-1, keepdims=True)
    o = jnp.dot(p_main.astype(BF16), v_ref[0: lp - LANES, :], preferred_element_type=F32)
    o = o + jnp.dot(p_tail.astype(BF16), v_ref[lp - LANES: lp, :], preferred_element_type=F32)
    return o, l


def _tail_bias(seq, lp):
    lane = lax.broadcasted_iota(jnp.int32, (1, LANES), 1)
    return jnp.where(lane < seq - (lp - LANES), 0.0, NEG_BIG).astype(F32)


def _nt_dot(a, b):
    return lax.dot_general(a, b, (((1,), (1,)), ((), ())), preferred_element_type=F32)


def _diff_attn_kernel(lq1_ref, lk1_ref, lq2_ref, lk2_ref, q_ref, k_ref, v_ref, g_ref, o_ref, kpad, vpad,
                      *, seq, lam_init, tq):
    lp = kpad.shape[0]
    kpad[0:seq, :] = k_ref[0]
    kpad[seq:lp, :] = jnp.zeros((lp - seq, LANES), BF16)
    vpad[0:seq, :] = v_ref[0]
    vpad[seq:lp, :] = jnp.zeros((lp - seq, LANES), BF16)
    lam = (jnp.exp(jnp.sum(lq1_ref[...] * lk1_ref[...], axis=-1, keepdims=True))
           - jnp.exp(jnp.sum(lq2_ref[...] * lk2_ref[...], axis=-1, keepdims=True)) + lam_init)
    tail_bias = _tail_bias(seq, lp)
    gain = g_ref[...] * (1.0 - lam_init)

    def attend(q):
        lane = lax.broadcasted_iota(jnp.int32, q.shape, 1)
        zero = jnp.zeros_like(q)
        o1, l1 = _softmax_pv(_nt_dot(jnp.where(lane < HEAD_DIM, q, zero), kpad[...]), tail_bias, vpad)
        o2, l2 = _softmax_pv(_nt_dot(jnp.where(lane >= HEAD_DIM, q, zero), kpad[...]), tail_bias, vpad)
        o = o1 / l1 - lam * (o2 / l2)
        return (o * lax.rsqrt(jnp.mean(o * o, axis=-1, keepdims=True) + EPS) * gain).astype(o_ref.dtype)

    o_ref[0, 0:N_META, :] = attend(q_ref[0, 0:N_META, :])

    def body(i, c):
        rows = pl.ds(pl.multiple_of(N_META + i * tq, 16), tq)
        o_ref[0, rows, :] = attend(q_ref[0, rows, :])
        return c

    lax.fori_loop(0, (seq - N_META) // tq, body, 0)


def _diff_attn(proj3, lq1, lk1, lq2, lk2, subln_g, lam_init, heads):
    nb, seq, _ = proj3.shape
    lp = pl.cdiv(seq, LANES) * LANES
    assert lp > seq
    tq = _tile(seq - N_META, 256, 16)
    vec = lambda: pl.BlockSpec((1, HEAD_DIM), lambda b, h: (0, 0))
    return pl.pallas_call(
        functools.partial(_diff_attn_kernel, seq=seq, lam_init=lam_init, tq=tq),
        grid=(nb, heads),
        in_specs=[
            vec(), vec(), vec(), vec(),
            pl.BlockSpec((1, seq, LANES), lambda b, h: (b, 0, h)),
            pl.BlockSpec((1, seq, LANES), lambda b, h: (b, 0, heads + h)),
            pl.BlockSpec((1, seq, LANES), lambda b, h: (b, 0, 2 * heads + h)),
            pl.BlockSpec((1, LANES), lambda b, h: (0, 0)),
        ],
        out_specs=pl.BlockSpec((1, seq, LANES), lambda b, h: (b, 0, h)),
        out_shape=jax.ShapeDtypeStruct((nb, seq, heads * LANES), BF16),
        scratch_shapes=[pltpu.VMEM((lp, LANES), BF16), pltpu.VMEM((lp, LANES), BF16)],
        compiler_params=_params("parallel", "parallel"),
        name="diff_attn",
    )(lq1, lk1, lq2, lk2, proj3, proj3, proj3, subln_g)


def _conv_kernel(gb_ref, gc_ref, cx_ref, w_ref, o_ref):
    u = gc_ref[0].astype(F32) * cx_ref[0].astype(F32)
    seq = u.shape[0]
    row = lax.broadcasted_iota(jnp.int32, u.shape, 0)
    prev = jnp.where(row == 0, 0.0, pltpu.roll(u, 1, 0))
    nxt = jnp.where(row == seq - 1, 0.0, pltpu.roll(u, seq - 1, 0))
    y = prev * w_ref[0:1, :] + u * w_ref[1:2, :] + nxt * w_ref[2:3, :]
    o_ref[0] = (gb_ref[0].astype(F32) * y).astype(o_ref.dtype)


def _gated_conv(proj3, conv_w, width):
    nb, seq, _ = proj3.shape
    tc = _tile(width, 256, LANES)
    nc = width // tc
    blk = lambda part: pl.BlockSpec((1, seq, tc), lambda b, c: (b, 0, part * nc + c))
    return pl.pallas_call(
        _conv_kernel,
        grid=(nb, nc),
        in_specs=[blk(3), blk(4), blk(5), pl.BlockSpec((3, tc), lambda b, c: (0, c))],
        out_specs=pl.BlockSpec((1, seq, tc), lambda b, c: (b, 0, c)),
        out_shape=jax.ShapeDtypeStruct((nb, seq, width), BF16),
        compiler_params=_params("parallel", "parallel"),
        name="gated_conv",
    )(proj3, proj3, proj3, conv_w)


def _out_proj_router_kernel(a1_ref, a2_ref, w_ref, h_ref, g_ref, wr_ref, hn_ref, tn_ref, aff_ref):
    k1 = a1_ref.shape[1]
    y = jnp.dot(a1_ref[...], w_ref[0:k1, :], preferred_element_type=F32)
    y = y + jnp.dot(a2_ref[...], w_ref[k1:, :], preferred_element_type=F32)
    h = h_ref[...] + y
    hn_ref[...] = h
    t = _rms(h, g_ref[...])
    tn_ref[...] = t
    logits = jnp.dot(t.astype(BF16), wr_ref[...], preferred_element_type=F32)
    e = jnp.exp(logits - jnp.max(logits, axis=-1, keepdims=True))
    aff_ref[...] = e / jnp.sum(e, axis=-1, keepdims=True)


def _out_proj_router(a1, a2, col1, col2, w_out, h, ffn_g, w_router):
    nt, d = h.shape
    k1 = w_out.shape[0] // 2
    n_exp = w_router.shape[1]
    tm = _tile(nt, 512, 16)
    row = lambda c: (lambda i: (i, c))
    const = lambda i: (0, 0)
    return pl.pallas_call(
        _out_proj_router_kernel,
        grid=(nt // tm,),
        in_specs=[
            pl.BlockSpec((tm, k1), row(col1)),
            pl.BlockSpec((tm, k1), row(col2)),
            pl.BlockSpec(w_out.shape, const),
            pl.BlockSpec((tm, d), row(0)),
            pl.BlockSpec((1, d), const),
            pl.BlockSpec(w_router.shape, const),
        ],
        out_specs=[
            pl.BlockSpec((tm, d), row(0)),
            pl.BlockSpec((tm, d), row(0)),
            pl.BlockSpec((tm, n_exp), row(0)),
        ],
        out_shape=[
            jax.ShapeDtypeStruct((nt, d), F32),
            jax.ShapeDtypeStruct((nt, d), F32),
            jax.ShapeDtypeStruct((nt, n_exp), F32),
        ],
        compiler_params=_params("parallel"),
        name="out_proj_router",
    )(a1, a2, w_out, h, ffn_g, w_router)


def _odd_in_kernel(x_ref, g_ref, w_ref, qg_ref, kvg_ref, cos_ref, sin_ref, cq_ref, ckv_ref, kr_ref):
    xn = _rms(x_ref[...], g_ref[...]).astype(BF16)
    p = jnp.dot(xn, w_ref[...], preferred_element_type=F32)
    qr, kvr = cq_ref.shape[1], ckv_ref.shape[1]
    cq_ref[...] = _rms(p[:, :qr], qg_ref[...]).astype(cq_ref.dtype)
    ckv_ref[...] = _rms(p[:, qr: qr + kvr], kvg_ref[...]).astype(ckv_ref.dtype)
    kr_ref[...] = _rope128(p[:, qr + kvr:], cos_ref[...], sin_ref[...]).astype(kr_ref.dtype)


def _odd_in(h, g, w_in_pad, q_g, kv_g, cos128, sin128, seq):
    nt, d = h.shape
    qr, kvr = q_g.shape[1], kv_g.shape[1]
    tm = _tile(seq, 768, 16)
    tiles_per_seq = seq // tm
    row = lambda i: (i, 0)
    const = lambda i: (0, 0)
    pos = lambda i: (i % tiles_per_seq, 0)
    return pl.pallas_call(
        _odd_in_kernel,
        grid=(nt // tm,),
        in_specs=[
            pl.BlockSpec((tm, d), row),
            pl.BlockSpec((1, d), const),
            pl.BlockSpec(w_in_pad.shape, const),
            pl.BlockSpec((1, qr), const),
            pl.BlockSpec((1, kvr), const),
            pl.BlockSpec((tm, LANES), pos),
            pl.BlockSpec((tm, LANES), pos),
        ],
        out_specs=[
            pl.BlockSpec((tm, qr), row),
            pl.BlockSpec((tm, kvr), row),
            pl.BlockSpec((tm, LANES), row),
        ],
        out_shape=[
            jax.ShapeDtypeStruct((nt, qr), BF16),
            jax.ShapeDtypeStruct((nt, kvr), BF16),
            jax.ShapeDtypeStruct((nt, LANES), BF16),
        ],
        compiler_params=_params("parallel"),
        name="odd_in",
    )(h, g, w_in_pad, q_g, kv_g, cos128, sin128)


def _up_proj_kernel(a_ref, w_ref, cos_ref, sin_ref, o_ref, *, rope, scale):
    a = a_ref[...]
    cos, sin = cos_ref[...], sin_ref[...]
    for c in range(o_ref.shape[1] // MLA_QK_PAD):
        lo = c * MLA_QK_PAD
        acc = jnp.dot(a, w_ref[:, lo: lo + MLA_QK_PAD], preferred_element_type=F32)
        if rope:
            o_ref[:, lo: lo + LANES] = (acc[:, :LANES] * scale).astype(o_ref.dtype)
            o_ref[:, lo + LANES: lo + MLA_QK_PAD] = (_rope128(acc[:, LANES:], cos, sin) * scale).astype(o_ref.dtype)
        else:
            o_ref[:, lo: lo + MLA_QK_PAD] = acc.astype(o_ref.dtype)


def _up_proj(a, w, cos128, sin128, seq, rope, scale):
    nt, k = a.shape
    n = w.shape[1]
    tm = _tile(seq, 768, 16)
    tiles_per_seq = seq // tm
    pos = lambda i: (i % tiles_per_seq, 0)
    return pl.pallas_call(
        functools.partial(_up_proj_kernel, rope=rope, scale=scale),
        grid=(nt // tm,),
        in_specs=[
            pl.BlockSpec((tm, k), lambda i: (i, 0)),
            pl.BlockSpec((k, n), lambda i: (0, 0)),
            pl.BlockSpec((tm, LANES), pos),
            pl.BlockSpec((tm, LANES), pos),
        ],
        out_specs=pl.BlockSpec((tm, n), lambda i: (i, 0)),
        out_shape=jax.ShapeDtypeStruct((nt, n), BF16),
        compiler_params=_params("parallel"),
        name="q_up_proj" if rope else "kv_up_proj",
    )(a, w, cos128, sin128)


def _mla_attn_kernel(q_ref, kn_ref, kr_ref, v_ref, o_ref, kcat, vpad, *, seq, tq):
    lp = kcat.shape[0]
    kcat[0:seq, 0:LANES] = kn_ref[0]
    kcat[0:seq, LANES:] = kr_ref[0]
    kcat[seq:lp, :] = jnp.zeros((lp - seq, MLA_QK_PAD), BF16)
    vpad[0:seq, :] = v_ref[0]
    vpad[seq:lp, :] = jnp.zeros((lp - seq, LANES), BF16)
    tail_bias = _tail_bias(seq, lp)

    def attend(q):
        o, l = _softmax_pv(_nt_dot(q, kcat[...]), tail_bias, vpad)
        return (o / l).astype(o_ref.dtype)

    o_ref[0, 0:N_META, :] = attend(q_ref[0, 0:N_META, :])

    def body(i, c):
        rows = pl.ds(pl.multiple_of(N_META + i * tq, 16), tq)
        o_ref[0, rows, :] = attend(q_ref[0, rows, :])
        return c

    lax.fori_loop(0, (seq - N_META) // tq, body, 0)


def _mla_attn(q3, kv3, kr3, heads):
    nb, seq, _ = q3.shape
    lp = pl.cdiv(seq, LANES) * LANES
    assert lp > seq
    tq = _tile(seq - N_META, 256, 16)
    return pl.pallas_call(
        functools.partial(_mla_attn_kernel, seq=seq, tq=tq),
        grid=(nb, heads),
        in_specs=[
            pl.BlockSpec((1, seq, MLA_QK_PAD), lambda b, h: (b, 0, h)),
            pl.BlockSpec((1, seq, LANES), lambda b, h: (b, 0, h)),
            pl.BlockSpec((1, seq, LANES), lambda b, h: (b, 0, 0)),
            pl.BlockSpec((1, seq, LANES), lambda b, h: (b, 0, heads + h)),
        ],
        out_specs=pl.BlockSpec((1, seq, LANES), lambda b, h: (b, 0, h)),
        out_shape=jax.ShapeDtypeStruct((nb, seq, heads * LANES), BF16),
        scratch_shapes=[pltpu.VMEM((lp, MLA_QK_PAD), BF16), pltpu.VMEM((lp, LANES), BF16)],
        compiler_params=_params("parallel", "parallel"),
        name="mla_attn",
    )(q3, kv3, kr3, kv3)


def _moe_ffn_kernel(idx_ref, tn_hbm, g_ref, wg_ref, wu_ref, wd_ref, o_ref, xs_buf, sem, *, tm, tiles_per_expert):
    base = (pl.program_id(0) * tiles_per_expert + pl.program_id(1)) * tm

    def issue(r, c):
        tok = idx_ref[base + r]
        pltpu.make_async_copy(tn_hbm.at[pl.ds(tok, 1), :], xs_buf.at[pl.ds(r, 1), :], sem).start()
        return c

    lax.fori_loop(0, tm, issue, 0, unroll=8)
    pltpu.make_async_copy(tn_hbm.at[pl.ds(0, tm), :], xs_buf, sem).wait()
    xs = xs_buf[...].astype(BF16)
    hg = jnp.dot(xs, wg_ref[...], preferred_element_type=F32)
    hu = jnp.dot(xs, wu_ref[...], preferred_element_type=F32)
    hid = (hg * jax.nn.sigmoid(hg) * hu).astype(BF16)
    o_ref[...] = jnp.dot(hid, wd_ref[...], preferred_element_type=F32) * g_ref[...]


def _moe_ffn(idx_flat, tn, g_col, w_gate, w_up, w_down, slots):
    nt, d = tn.shape
    n_exp, _, ff = w_gate.shape
    tm = _tile(slots, 384, 8)
    tiles = slots // tm
    grid_spec = pltpu.PrefetchScalarGridSpec(
        num_scalar_prefetch=1,
        grid=(n_exp, tiles),
        in_specs=[
            pl.BlockSpec(memory_space=pl.ANY),
            pl.BlockSpec((tm, 1), lambda e, m, idx: (e * tiles + m, 0)),
            pl.BlockSpec((None, d, ff), lambda e, m, idx: (e, 0, 0)),
            pl.BlockSpec((None, d, ff), lambda e, m, idx: (e, 0, 0)),
            pl.BlockSpec((None, ff, d), lambda e, m, idx: (e, 0, 0)),
        ],
        out_specs=pl.BlockSpec((tm, d), lambda e, m, idx: (e * tiles + m, 0)),
        scratch_shapes=[pltpu.VMEM((tm, d), F32), pltpu.SemaphoreType.DMA],
    )
    return pl.pallas_call(
        functools.partial(_moe_ffn_kernel, tm=tm, tiles_per_expert=tiles),
        grid_spec=grid_spec,
        out_shape=jax.ShapeDtypeStruct((n_exp * slots, d), F32),
        compiler_params=_params("arbitrary", "arbitrary"),
        name="moe_ffn",
    )(idx_flat, tn, g_col, w_gate, w_up, w_down)


def _combine_kernel(pairs_ref, off_ref, kmax_ref, ye_hbm, h_ref, cnt_ref, o_ref, planes, sem):
    i = pl.program_id(0)

    @pl.when(i == 0)
    def _():
        planes[...] = jnp.zeros(planes.shape, F32)

    n0, n1 = off_ref[i], off_ref[i + 1]

    def issue(p, c):
        word = pairs_ref[p]
        row = word >> PAIR_ROW_SHIFT
        k = (word >> PAIR_PLANE_SHIFT) & (MAX_PLANES - 1)
        t = word & ((1 << PAIR_PLANE_SHIFT) - 1)
        pltpu.make_async_copy(ye_hbm.at[pl.ds(row, 1), :], planes.at[k, pl.ds(t, 1), :], sem).start()
        return c

    lax.fori_loop(n0, n1, issue, 0)

    def wait(p, c):
        pltpu.make_async_copy(ye_hbm.at[pl.ds(0, 1), :], planes.at[0, pl.ds(0, 1), :], sem).wait()
        return c

    lax.fori_loop(n0, n1, wait, 0)
    cnt = cnt_ref[...]
    o_ref[...] = h_ref[...]

    def add_plane(k, c):
        o_ref[...] += jnp.where(cnt > k, planes[k], 0.0)
        return c

    lax.fori_loop(0, kmax_ref[i], add_plane, 0)


def _combine(pairs, off, kmax, ye, h, cnt_col, t_tile):
    nt, d = h.shape
    grid_spec = pltpu.PrefetchScalarGridSpec(
        num_scalar_prefetch=3,
        grid=(nt // t_tile,),
        in_specs=[
            pl.BlockSpec(memory_space=pl.ANY),
            pl.BlockSpec((t_tile, d), lambda i, *_: (i, 0)),
            pl.BlockSpec((t_tile, 1), lambda i, *_: (i, 0)),
        ],
        out_specs=pl.BlockSpec((t_tile, d), lambda i, *_: (i, 0)),
        scratch_shapes=[pltpu.VMEM((MAX_PLANES, t_tile, d), F32), pltpu.SemaphoreType.DMA],
    )
    return pl.pallas_call(
        _combine_kernel,
        grid_spec=grid_spec,
        out_shape=jax.ShapeDtypeStruct((nt, d), F32),
        compiler_params=_params("arbitrary"),
        name="moe_combine",
    )(pairs, off, kmax, ye, h, cnt_col)


def _combine_plan(idx_all, nt, t_tile):
    n_rows = idx_all.size
    t_flat = idx_all.reshape(-1)
    order = jnp.argsort(t_flat).astype(jnp.int32)
    t_sorted = t_flat[order]
    cnt = jnp.zeros((nt,), jnp.int32).at[t_flat].add(1)
    start = jnp.cumsum(cnt) - cnt
    plane = jnp.arange(n_rows, dtype=jnp.int32) - start[t_sorted]
    pairs = (order << PAIR_ROW_SHIFT) | (plane << PAIR_PLANE_SHIFT) | (t_sorted % t_tile)
    off = jnp.concatenate([start[::t_tile], jnp.full((1,), n_rows, jnp.int32)])
    kmax = jnp.max(cnt.reshape(nt // t_tile, t_tile), axis=1)
    return pairs, off, kmax, cnt[:, None]


def _moe(h, tn, aff, group_sizes, w_gate, w_up, w_down):
    nt, _ = h.shape
    n_exp = aff.shape[1]
    idx_parts, g_parts, base = [], [], 0
    for n in group_sizes:
        cap = N_EXPERT_CAPACITY_FACTOR * n // n_exp
        g, idx = lax.top_k(aff[base: base + n].T, cap)
        idx_parts.append(idx.astype(jnp.int32) + base)
        g_parts.append(g)
        base += n
    idx_all = jnp.concatenate(idx_parts, axis=1)
    g_all = jnp.concatenate(g_parts, axis=1)
    slots = idx_all.shape[1]
    assert (n_exp * slots) < (1 << (31 - PAIR_ROW_SHIFT))
    ye = _moe_ffn(idx_all.reshape(-1), tn, g_all.reshape(-1, 1), w_gate, w_up, w_down, slots)
    t_tile = _tile(nt, 160, 8)
    pairs, off, kmax, cnt_col = _combine_plan(idx_all, nt, t_tile)
    return _combine(pairs, off, kmax, ye, h, cnt_col, t_tile)


def _final_norm_kernel(x_ref, g_ref, o_ref):
    o_ref[...] = _rms(x_ref[...], g_ref[...])


def _final_norm(h, g):
    nt, d = h.shape
    tm = _tile(nt, 512, 8)
    return pl.pallas_call(
        _final_norm_kernel,
        grid=(nt // tm,),
        in_specs=[pl.BlockSpec((tm, d), lambda i: (i, 0)), pl.BlockSpec((1, d), lambda i: (0, 0))],
        out_specs=pl.BlockSpec((tm, d), lambda i: (i, 0)),
        out_shape=jax.ShapeDtypeStruct((nt, d), F32),
        compiler_params=_params("parallel"),
        name="final_norm",
    )(h, g)


def _lambda_init(layer):
    import math
    return 0.8 - 0.6 * math.exp(-0.3 * layer)


def kernel(x_prompt, x_sample, meta_tokens, mix_norm_g, ffn_norm_g, final_norm_g, ev_w_in, ev_w_out, ev_lambda_q1, ev_lambda_k1, ev_lambda_q2, ev_lambda_k2, ev_subln_g, ev_conv_w, od_w_in, od_q_norm_g, od_w_q_b, od_kv_norm_g, od_w_kv_b, od_w_out, moe_w_router, moe_w_gate, moe_w_up, moe_w_down):
    d = x_prompt.shape[-1]
    seq = N_META + x_prompt.shape[1]
    assert x_sample.shape[1] == x_prompt.shape[1]
    nb_p, nb_s = x_prompt.shape[0], x_sample.shape[0]
    nb = nb_p + nb_s
    nt = nb * seq
    group_sizes = (nb_p * seq, nb_s * seq)
    depth = mix_norm_g.shape[0]

    x = jnp.concatenate([x_prompt, x_sample], axis=0)
    meta = jnp.broadcast_to(meta_tokens[None].astype(x.dtype), (nb, N_META, d))
    h = jnp.concatenate([meta, x], axis=1).reshape(nt, d)
    cos128, sin128 = _rope_tables(seq)

    for layer in range(depth):
        i = layer // 2
        mix_g = mix_norm_g[layer][None]
        ffn_g = ffn_norm_g[layer][None]
        w_router = moe_w_router[layer].astype(BF16)
        if layer % 2 == 0:
            width = d // 2
            heads = width // LANES
            proj = _even_proj(h, mix_g, ev_w_in[i].astype(BF16), cos128, sin128, seq)
            proj3 = proj.reshape(nb, seq, 6 * width)
            attn = _diff_attn(proj3, ev_lambda_q1[i][None], ev_lambda_k1[i][None], ev_lambda_q2[i][None],
                              ev_lambda_k2[i][None], ev_subln_g[i][None], _lambda_init(layer), heads)
            conv = _gated_conv(proj3, ev_conv_w[i], width)
            h, tn, aff = _out_proj_router(attn.reshape(nt, width), conv.reshape(nt, width), 0, 0,
                                          ev_w_out[i].astype(BF16), h, ffn_g, w_router)
        else:
            q_rank, kv_rank = od_q_norm_g.shape[1], od_kv_norm_g.shape[1]
            heads = od_w_q_b.shape[2] // (MLA_NOPE + MLA_ROPE)
            w_in_pad = jnp.pad(od_w_in[i], ((0, 0), (0, LANES - MLA_ROPE))).astype(BF16)
            w_q = od_w_q_b[i].reshape(q_rank, heads, MLA_NOPE + MLA_ROPE)
            w_q = jnp.pad(w_q, ((0, 0), (0, 0), (0, MLA_QK_PAD - MLA_NOPE - MLA_ROPE)))
            w_q = w_q.reshape(q_rank, heads * MLA_QK_PAD).astype(BF16)
            w_kv = od_w_kv_b[i].reshape(kv_rank, heads, MLA_NOPE + MLA_V)
            w_kv = jnp.concatenate([w_kv[:, :, :MLA_NOPE].reshape(kv_rank, heads * MLA_NOPE),
                                    w_kv[:, :, MLA_NOPE:].reshape(kv_rank, heads * MLA_V)], axis=1).astype(BF16)
            cq, ckv, kr = _odd_in(h, mix_g, w_in_pad, od_q_norm_g[i][None], od_kv_norm_g[i][None], cos128, sin128, seq)
            q = _up_proj(cq, w_q, cos128, sin128, seq, True, (MLA_NOPE + MLA_ROPE) ** -0.5)
            kv = _up_proj(ckv, w_kv, cos128, sin128, seq, False, 1.0)
            o = _mla_attn(q.reshape(nb, seq, heads * MLA_QK_PAD), kv.reshape(nb, seq, 2 * heads * LANES),
                          kr.reshape(nb, seq, LANES), heads)
            o = o.reshape(nt, heads * MLA_V)
            h, tn, aff = _out_proj_router(o, o, 0, 1, od_w_out[i].astype(BF16), h, ffn_g, w_router)
        h = _moe(h, tn, aff, group_sizes, moe_w_gate[layer].astype(BF16), moe_w_up[layer].astype(BF16),
                 moe_w_down[layer].astype(BF16))

    y = _final_norm(h, final_norm_g[None]).reshape(nb, seq, d)[:, N_META:]
    return (y[:nb_p], y[nb_p:])
```

```python
import functools
import math

import jax
import jax.numpy as jnp
from jax import lax
from jax.experimental import pallas as pl
from jax.experimental.pallas import tpu as pltpu

N_META = 16
EPS = 1e-6
ROPE_THETA = 10000.0
HEAD_DIM = 64
LANES = 128
MXU_COLS = 256
MLA_NOPE = 128
MLA_ROPE = 64
MLA_V = 128
MLA_QK_PAD = 256
ATTN_Q_TILE_MAX = 768
N_EXPERT_CAPACITY_FACTOR = 2
VMEM_LIMIT_BYTES = 56 * 1024 * 1024
NEG_BIG = -1e30
PAIR_ROW_SHIFT = 12
PAIR_CHUNK = 8
MAX_PLANES = 16

F32 = jnp.float32
BF16 = jnp.bfloat16


def _tile(n, cap, mult):
    best = None
    for t in range(mult, min(n, cap) + 1, mult):
        if n % t == 0:
            best = t
    assert best is not None, (n, cap, mult)
    return best


def _params(*sem):
    return pltpu.CompilerParams(dimension_semantics=sem, vmem_limit_bytes=VMEM_LIMIT_BYTES)


def _rms(x, g):
    return x * lax.rsqrt(jnp.mean(x * x, axis=-1, keepdims=True) + EPS) * g


def _rope128(x, cos, sin_signed):
    lane = lax.broadcasted_iota(jnp.int32, x.shape, 1)
    first_half = (lane % HEAD_DIM) < (HEAD_DIM // 2)
    rot = jnp.where(first_half, pltpu.roll(x, LANES - HEAD_DIM // 2, 1), pltpu.roll(x, HEAD_DIM // 2, 1))
    return x * cos + rot * sin_signed


def _rope_tables(length):
    inv = 1.0 / (ROPE_THETA ** (jnp.arange(0, HEAD_DIM, 2, dtype=F32) / HEAD_DIM))
    ang = jnp.arange(length, dtype=F32)[:, None] * inv[None, :]
    cos, sin = jnp.cos(ang), jnp.sin(ang)
    cos128 = jnp.tile(cos, (1, LANES // (HEAD_DIM // 2)))
    sin128 = jnp.tile(jnp.concatenate([-sin, sin], axis=1), (1, LANES // HEAD_DIM))
    return cos128, sin128


def _even_proj_kernel(x_ref, g_ref, w_ref, cos_ref, sin_ref, o_ref, xn_ref, *, rope, q_scale):
    j = pl.program_id(1)

    @pl.when(j == 0)
    def _():
        xn_ref[...] = _rms(x_ref[...], g_ref[...]).astype(BF16)

    xn = xn_ref[...]
    if rope:
        scale = jnp.where(j == 0, q_scale, 1.0).astype(F32)
        cos, sin = cos_ref[...], sin_ref[...]
    for c in range(o_ref.shape[1] // MXU_COLS):
        lo = c * MXU_COLS
        acc = jnp.dot(xn, w_ref[:, lo: lo + MXU_COLS], preferred_element_type=F32)
        if rope:
            for half in range(MXU_COLS // LANES):
                sl = slice(half * LANES, (half + 1) * LANES)
                o_ref[:, lo + half * LANES: lo + (half + 1) * LANES] = (
                    _rope128(acc[:, sl], cos, sin) * scale).astype(o_ref.dtype)
        else:
            o_ref[:, lo: lo + MXU_COLS] = acc.astype(o_ref.dtype)


def _even_proj(h, g, w_in, cos128, sin128, seq, rope):
    nt, d = h.shape
    tn = w_in.shape[1] // 6
    col0, n_cols = (0, 2) if rope else (2, 4)
    tm = _tile(seq, 768, 16)
    tiles_per_seq = seq // tm
    return pl.pallas_call(
        functools.partial(_even_proj_kernel, rope=rope, q_scale=HEAD_DIM ** -0.5),
        grid=(nt // tm, n_cols),
        in_specs=[
            pl.BlockSpec((tm, d), lambda i, j: (i, 0)),
            pl.BlockSpec((1, d), lambda i, j: (0, 0)),
            pl.BlockSpec((d, tn), lambda i, j: (0, col0 + j)),
            pl.BlockSpec((tm, LANES), lambda i, j: (i % tiles_per_seq, 0)),
            pl.BlockSpec((tm, LANES), lambda i, j: (i % tiles_per_seq, 0)),
        ],
        out_specs=pl.BlockSpec((tm, tn), lambda i, j: (i, j)),
        out_shape=jax.ShapeDtypeStruct((nt, n_cols * tn), BF16),
        scratch_shapes=[pltpu.VMEM((tm, d), BF16)],
        compiler_params=_params("parallel", "arbitrary"),
        name="even_proj_qk" if rope else "even_proj_rest",
    )(h, g, w_in, cos128, sin128)


def _store_scores(q, k_ref, tail_bias, s_ref):
    s = _nt_dot(q, k_ref[...])
    n_main = s.shape[1] - LANES
    s_ref[:, :n_main] = s[:, :n_main]
    s_ref[:, n_main:] = s[:, n_main:] + tail_bias


def _store_probs(s_ref, p_ref):
    s = s_ref[...]
    p = jnp.exp(s - jnp.max(s, axis=-1, keepdims=True))
    p_ref[...] = p.astype(p_ref.dtype)
    return jnp.sum(p, axis=-1, keepdims=True)


def _tail_bias(seq, lp):
    lane = lax.broadcasted_iota(jnp.int32, (1, LANES), 1)
    return jnp.where(lane < seq - (lp - LANES), 0.0, NEG_BIG).astype(F32)


def _nt_dot(a, b):
    return lax.dot_general(a, b, (((1,), (1,)), ((), ())), preferred_element_type=F32)


def _attention_pipeline(n_tiles, scores, softmax, pv):
    sums = {}
    for step in range(n_tiles + 2):
        if 0 <= step - 2 < n_tiles:
            pv(step - 2, (step - 2) % 2, sums.pop(step - 2))
        if 0 <= step - 1 < n_tiles:
            sums[step - 1] = softmax((step - 1) % 2)
        if step < n_tiles:
            scores(step, step % 2)


def _diff_attn_kernel(lq1_ref, lk1_ref, lq2_ref, lk2_ref, q_ref, k_ref, v_ref, g_ref, o_ref, kpad, vpad,
                      s1a, s1b, s2a, s2b, p1a, p1b, p2a, p2b, *, seq, lam_init, tq):
    lp = kpad.shape[0]
    kpad[0:seq, :] = k_ref[0]
    kpad[seq:lp, :] = jnp.zeros((lp - seq, LANES), BF16)
    vpad[0:seq, :] = v_ref[0]
    vpad[seq:lp, :] = jnp.zeros((lp - seq, LANES), BF16)
    lam = (jnp.exp(jnp.sum(lq1_ref[...] * lk1_ref[...], axis=-1, keepdims=True))
           - jnp.exp(jnp.sum(lq2_ref[...] * lk2_ref[...], axis=-1, keepdims=True)) + lam_init)
    tail_bias = _tail_bias(seq, lp)
    gain = g_ref[...] * (1.0 - lam_init)
    s_bufs, p_bufs = ((s1a, s2a), (s1b, s2b)), ((p1a, p2a), (p1b, p2b))

    def scores(t, par):
        q = q_ref[0, t * tq:(t + 1) * tq, :]
        lane = lax.broadcasted_iota(jnp.int32, q.shape, 1)
        zero = jnp.zeros_like(q)
        _store_scores(jnp.where(lane < HEAD_DIM, q, zero), kpad, tail_bias, s_bufs[par][0])
        _store_scores(jnp.where(lane >= HEAD_DIM, q, zero), kpad, tail_bias, s_bufs[par][1])

    def softmax(par):
        return [_store_probs(s_bufs[par][m], p_bufs[par][m]) for m in range(2)]

    def pv(t, par, l):
        o1 = jnp.dot(p_bufs[par][0][...], vpad[...], preferred_element_type=F32) / l[0]
        o2 = jnp.dot(p_bufs[par][1][...], vpad[...], preferred_element_type=F32) / l[1]
        o = o1 - lam * o2
        o = o * lax.rsqrt(jnp.mean(o * o, axis=-1, keepdims=True) + EPS) * gain
        o_ref[0, t * tq:(t + 1) * tq, :] = o.astype(o_ref.dtype)

    _attention_pipeline(seq // tq, scores, softmax, pv)


def _diff_attn(qk3, rest3, lq1, lk1, lq2, lk2, subln_g, lam_init, heads):
    nb, seq, _ = qk3.shape
    lp = pl.cdiv(seq, LANES) * LANES
    assert lp > seq
    tq = _tile(seq, ATTN_Q_TILE_MAX, 16)
    vec = lambda: pl.BlockSpec((1, HEAD_DIM), lambda b, h: (0, 0))
    return pl.pallas_call(
        functools.partial(_diff_attn_kernel, seq=seq, lam_init=lam_init, tq=tq),
        grid=(nb, heads),
        in_specs=[
            vec(), vec(), vec(), vec(),
            pl.BlockSpec((1, seq, LANES), lambda b, h: (b, 0, h)),
            pl.BlockSpec((1, seq, LANES), lambda b, h: (b, 0, heads + h)),
            pl.BlockSpec((1, seq, LANES), lambda b, h: (b, 0, h)),
            pl.BlockSpec((1, LANES), lambda b, h: (0, 0)),
        ],
        out_specs=pl.BlockSpec((1, seq, LANES), lambda b, h: (b, 0, h)),
        out_shape=jax.ShapeDtypeStruct((nb, seq, heads * LANES), BF16),
        scratch_shapes=[pltpu.VMEM((lp, LANES), BF16), pltpu.VMEM((lp, LANES), BF16)]
        + [pltpu.VMEM((tq, lp), F32)] * 4 + [pltpu.VMEM((tq, lp), BF16)] * 4,
        compiler_params=_params("parallel", "parallel"),
        name="diff_attn",
    )(lq1, lk1, lq2, lk2, qk3, qk3, rest3, subln_g)


def _conv_kernel(gb_ref, gc_ref, cx_ref, w_ref, o_ref):
    u = gc_ref[0].astype(F32) * cx_ref[0].astype(F32)
    seq = u.shape[0]
    row = lax.broadcasted_iota(jnp.int32, u.shape, 0)
    prev = jnp.where(row == 0, 0.0, pltpu.roll(u, 1, 0))
    nxt = jnp.where(row == seq - 1, 0.0, pltpu.roll(u, seq - 1, 0))
    y = prev * w_ref[0:1, :] + u * w_ref[1:2, :] + nxt * w_ref[2:3, :]
    o_ref[0] = (gb_ref[0].astype(F32) * y).astype(o_ref.dtype)


def _gated_conv(rest3, conv_w, width):
    nb, seq, _ = rest3.shape
    tc = _tile(width, 256, LANES)
    nc = width // tc
    blk = lambda part: pl.BlockSpec((1, seq, tc), lambda b, c: (b, 0, part * nc + c))
    return pl.pallas_call(
        _conv_kernel,
        grid=(nb, nc),
        in_specs=[blk(1), blk(2), blk(3), pl.BlockSpec((3, tc), lambda b, c: (0, c))],
        out_specs=pl.BlockSpec((1, seq, tc), lambda b, c: (b, 0, c)),
        out_shape=jax.ShapeDtypeStruct((nb, seq, width), BF16),
        compiler_params=_params("parallel", "parallel"),
        name="gated_conv",
    )(rest3, rest3, rest3, conv_w)


def _out_proj_router_kernel(a1_ref, a2_ref, w_ref, h_ref, g_ref, wr_ref, hn_ref, tn_ref, aff_ref):
    k1 = a1_ref.shape[1]
    y = jnp.dot(a1_ref[...], w_ref[0:k1, :], preferred_element_type=F32)
    y = y + jnp.dot(a2_ref[...], w_ref[k1:, :], preferred_element_type=F32)
    h = h_ref[...] + y
    hn_ref[...] = h
    t = _rms(h, g_ref[...])
    tn_ref[...] = t
    logits = jnp.dot(t.astype(BF16), wr_ref[...], preferred_element_type=F32)
    e = jnp.exp(logits - jnp.max(logits, axis=-1, keepdims=True))
    aff_ref[...] = e / jnp.sum(e, axis=-1, keepdims=True)


def _out_proj_router(a1, a2, col1, col2, w_out, h, ffn_g, w_router):
    nt, d = h.shape
    k1 = w_out.shape[0] // 2
    n_exp = w_router.shape[1]
    tm = _tile(nt, 512, 16)
    row = lambda c: (lambda i: (i, c))
    const = lambda i: (0, 0)
    return pl.pallas_call(
        _out_proj_router_kernel,
        grid=(nt // tm,),
        in_specs=[
            pl.BlockSpec((tm, k1), row(col1)),
            pl.BlockSpec((tm, k1), row(col2)),
            pl.BlockSpec(w_out.shape, const),
            pl.BlockSpec((tm, d), row(0)),
            pl.BlockSpec((1, d), const),
            pl.BlockSpec(w_router.shape, const),
        ],
        out_specs=[
            pl.BlockSpec((tm, d), row(0)),
            pl.BlockSpec((tm, d), row(0)),
            pl.BlockSpec((tm, n_exp), row(0)),
        ],
        out_shape=[
            jax.ShapeDtypeStruct((nt, d), F32),
            jax.ShapeDtypeStruct((nt, d), F32),
            jax.ShapeDtypeStruct((nt, n_exp), F32),
        ],
        compiler_params=_params("parallel"),
        name="out_proj_router",
    )(a1, a2, w_out, h, ffn_g, w_router)


def _odd_in_kernel(x_ref, g_ref, w_ref, qg_ref, kvg_ref, cos_ref, sin_ref, cq_ref, ckv_ref, kr_ref):
    xn = _rms(x_ref[...], g_ref[...]).astype(BF16)
    p = jnp.dot(xn, w_ref[...], preferred_element_type=F32)
    qr, kvr = cq_ref.shape[1], ckv_ref.shape[1]
    cq_ref[...] = _rms(p[:, :qr], qg_ref[...]).astype(cq_ref.dtype)
    ckv_ref[...] = _rms(p[:, qr: qr + kvr], kvg_ref[...]).astype(ckv_ref.dtype)
    kr_ref[...] = _rope128(p[:, qr + kvr:], cos_ref[...], sin_ref[...]).astype(kr_ref.dtype)


def _odd_in(h, g, w_in_pad, q_g, kv_g, cos128, sin128, seq):
    nt, d = h.shape
    qr, kvr = q_g.shape[1], kv_g.shape[1]
    tm = _tile(seq, 768, 16)
    tiles_per_seq = seq // tm
    row = lambda i: (i, 0)
    const = lambda i: (0, 0)
    pos = lambda i: (i % tiles_per_seq, 0)
    return pl.pallas_call(
        _odd_in_kernel,
        grid=(nt // tm,),
        in_specs=[
            pl.BlockSpec((tm, d), row),
            pl.BlockSpec((1, d), const),
            pl.BlockSpec(w_in_pad.shape, const),
            pl.BlockSpec((1, qr), const),
            pl.BlockSpec((1, kvr), const),
            pl.BlockSpec((tm, LANES), pos),
            pl.BlockSpec((tm, LANES), pos),
        ],
        out_specs=[
            pl.BlockSpec((tm, qr), row),
            pl.BlockSpec((tm, kvr), row),
            pl.BlockSpec((tm, LANES), row),
        ],
        out_shape=[
            jax.ShapeDtypeStruct((nt, qr), BF16),
            jax.ShapeDtypeStruct((nt, kvr), BF16),
            jax.ShapeDtypeStruct((nt, LANES), BF16),
        ],
        compiler_params=_params("parallel"),
        name="odd_in",
    )(h, g, w_in_pad, q_g, kv_g, cos128, sin128)


def _up_proj_kernel(a_ref, w_ref, cos_ref, sin_ref, o_ref, *, rope, scale):
    a = a_ref[...]
    cos, sin = cos_ref[...], sin_ref[...]
    for c in range(o_ref.shape[1] // MLA_QK_PAD):
        lo = c * MLA_QK_PAD
        acc = jnp.dot(a, w_ref[:, lo: lo + MLA_QK_PAD], preferred_element_type=F32)
        if rope:
            o_ref[:, lo: lo + LANES] = (acc[:, :LANES] * scale).astype(o_ref.dtype)
            o_ref[:, lo + LANES: lo + MLA_QK_PAD] = (_rope128(acc[:, LANES:], cos, sin) * scale).astype(o_ref.dtype)
        else:
            o_ref[:, lo: lo + MLA_QK_PAD] = acc.astype(o_ref.dtype)


def _up_proj(a, w, cos128, sin128, seq, rope, scale):
    nt, k = a.shape
    n = w.shape[1]
    tm = _tile(seq, 768, 16)
    tiles_per_seq = seq // tm
    pos = lambda i: (i % tiles_per_seq, 0)
    return pl.pallas_call(
        functools.partial(_up_proj_kernel, rope=rope, scale=scale),
        grid=(nt // tm,),
        in_specs=[
            pl.BlockSpec((tm, k), lambda i: (i, 0)),
            pl.BlockSpec((k, n), lambda i: (0, 0)),
            pl.BlockSpec((tm, LANES), pos),
            pl.BlockSpec((tm, LANES), pos),
        ],
        out_specs=pl.BlockSpec((tm, n), lambda i: (i, 0)),
        out_shape=jax.ShapeDtypeStruct((nt, n), BF16),
        compiler_params=_params("parallel"),
        name="q_up_proj" if rope else "kv_up_proj",
    )(a, w, cos128, sin128)


def _mla_attn_kernel(q_ref, kn_ref, kr_ref, v_ref, o_ref, kcat, vpad, sa, sb, pa, pb, *, seq, tq):
    lp = kcat.shape[0]
    kcat[0:seq, 0:LANES] = kn_ref[0]
    kcat[0:seq, LANES:] = kr_ref[0]
    kcat[seq:lp, :] = jnp.zeros((lp - seq, MLA_QK_PAD), BF16)
    vpad[0:seq, :] = v_ref[0]
    vpad[seq:lp, :] = jnp.zeros((lp - seq, LANES), BF16)
    tail_bias = _tail_bias(seq, lp)

    s_bufs, p_bufs = (sa, sb), (pa, pb)

    def scores(t, par):
        _store_scores(q_ref[0, t * tq:(t + 1) * tq, :], kcat, tail_bias, s_bufs[par])

    def softmax(par):
        return _store_probs(s_bufs[par], p_bufs[par])

    def pv(t, par, l):
        o = jnp.dot(p_bufs[par][...], vpad[...], preferred_element_type=F32) / l
        o_ref[0, t * tq:(t + 1) * tq, :] = o.astype(o_ref.dtype)

    _attention_pipeline(seq // tq, scores, softmax, pv)


def _mla_attn(q3, kv3, kr3, heads):
    nb, seq, _ = q3.shape
    lp = pl.cdiv(seq, LANES) * LANES
    assert lp > seq
    tq = _tile(seq, ATTN_Q_TILE_MAX, 16)
    return pl.pallas_call(
        functools.partial(_mla_attn_kernel, seq=seq, tq=tq),
        grid=(nb, heads),
        in_specs=[
            pl.BlockSpec((1, seq, MLA_QK_PAD), lambda b, h: (b, 0, h)),
            pl.BlockSpec((1, seq, LANES), lambda b, h: (b, 0, h)),
            pl.BlockSpec((1, seq, LANES), lambda b, h: (b, 0, 0)),
            pl.BlockSpec((1, seq, LANES), lambda b, h: (b, 0, heads + h)),
        ],
        out_specs=pl.BlockSpec((1, seq, LANES), lambda b, h: (b, 0, h)),
        out_shape=jax.ShapeDtypeStruct((nb, seq, heads * LANES), BF16),
        scratch_shapes=[pltpu.VMEM((lp, MLA_QK_PAD), BF16), pltpu.VMEM((lp, LANES), BF16)]
        + [pltpu.VMEM((tq, lp), F32)] * 2 + [pltpu.VMEM((tq, lp), BF16)] * 2,
        compiler_params=_params("parallel", "parallel"),
        name="mla_attn",
    )(q3, kv3, kr3, kv3)


def _moe_ffn_kernel(idx_ref, tn_hbm, g_ref, wg_ref, wu_ref, wd_ref, o_ref, xs_buf, sem, *, tm):
    step = pl.program_id(0) * pl.num_programs(1) + pl.program_id(1)
    n_steps = pl.num_programs(0) * pl.num_programs(1)
    slot = step % 2

    def gather(tile, into):
        for r in range(tm):
            tok = idx_ref[tile * tm + r]
            pltpu.make_async_copy(tn_hbm.at[pl.ds(tok, 1), :], xs_buf.at[into, pl.ds(r, 1), :], sem.at[into]).start()

    def wait(which):
        pltpu.make_async_copy(tn_hbm.at[pl.ds(0, tm), :], xs_buf.at[which], sem.at[which]).wait()

    @pl.when(step == 0)
    def _():
        gather(0, 0)

    wait(slot)
    xs = xs_buf[slot].astype(BF16)
    hg = jnp.dot(xs, wg_ref[...], preferred_element_type=F32)
    hu = jnp.dot(xs, wu_ref[...], preferred_element_type=F32)
    hid = (hg * jax.nn.sigmoid(hg) * hu).astype(BF16)
    gather(step + 1, 1 - slot)
    o_ref[...] = jnp.dot(hid, wd_ref[...], preferred_element_type=F32) * g_ref[...]

    @pl.when(step == n_steps - 1)
    def _():
        wait(1 - slot)


def _moe_ffn(idx_flat, tn, g_col, w_gate, w_up, w_down, slots):
    nt, d = tn.shape
    n_exp, _, ff = w_gate.shape
    tm = _tile(slots, 384, 8)
    tiles = slots // tm
    grid_spec = pltpu.PrefetchScalarGridSpec(
        num_scalar_prefetch=1,
        grid=(n_exp, tiles),
        in_specs=[
            pl.BlockSpec(memory_space=pl.ANY),
            pl.BlockSpec((tm, 1), lambda e, m, idx: (e * tiles + m, 0)),
            pl.BlockSpec((None, d, ff), lambda e, m, idx: (e, 0, 0)),
            pl.BlockSpec((None, d, ff), lambda e, m, idx: (e, 0, 0)),
            pl.BlockSpec((None, ff, d), lambda e, m, idx: (e, 0, 0)),
        ],
        out_specs=pl.BlockSpec((tm, d), lambda e, m, idx: (e * tiles + m, 0)),
        scratch_shapes=[pltpu.VMEM((2, tm, d), F32), pltpu.SemaphoreType.DMA((2,))],
    )
    idx_padded = jnp.concatenate([idx_flat, idx_flat[:tm]])
    return pl.pallas_call(
        functools.partial(_moe_ffn_kernel, tm=tm),
        grid_spec=grid_spec,
        out_shape=jax.ShapeDtypeStruct((n_exp * slots, d), F32),
        compiler_params=_params("arbitrary", "arbitrary"),
        name="moe_ffn",
    )(idx_padded, tn, g_col, w_gate, w_up, w_down)


def _combine_kernel(pairs_ref, off_ref, kmax_ref, ye_hbm, h_ref, cnt_ref, o_ref, planes, sem):
    i = pl.program_id(0)
    t_tile = h_ref.shape[0]

    @pl.when(i == 0)
    def _():
        planes[...] = jnp.zeros(planes.shape, F32)

    c0, c1 = off_ref[i], off_ref[i + 1]

    def issue(c, carry):
        for u in range(PAIR_CHUNK):
            word = pairs_ref[c * PAIR_CHUNK + u]
            row = word >> PAIR_ROW_SHIFT
            dst = word & ((1 << PAIR_ROW_SHIFT) - 1)
            pltpu.make_async_copy(ye_hbm.at[pl.ds(row, 1), :], planes.at[pl.ds(dst, 1), :], sem).start()
        return carry

    lax.fori_loop(c0, c1, issue, 0)

    def wait(c, carry):
        pltpu.make_async_copy(ye_hbm.at[pl.ds(0, PAIR_CHUNK), :], planes.at[pl.ds(0, PAIR_CHUNK), :], sem).wait()
        return carry

    lax.fori_loop(c0, c1, wait, 0)
    cnt = cnt_ref[...]
    o_ref[...] = h_ref[...]

    def add_plane(k, carry):
        plane = planes[pl.ds(pl.multiple_of(k * t_tile, 8), t_tile), :]
        o_ref[...] += jnp.where(cnt > k, plane, 0.0)
        return carry

    lax.fori_loop(0, kmax_ref[i], add_plane, 0)


def _combine(pairs, off, kmax, ye, h, cnt_col, t_tile):
    nt, d = h.shape
    grid_spec = pltpu.PrefetchScalarGridSpec(
        num_scalar_prefetch=3,
        grid=(nt // t_tile,),
        in_specs=[
            pl.BlockSpec(memory_space=pl.ANY),
            pl.BlockSpec((t_tile, d), lambda i, *_: (i, 0)),
            pl.BlockSpec((t_tile, 1), lambda i, *_: (i, 0)),
        ],
        out_specs=pl.BlockSpec((t_tile, d), lambda i, *_: (i, 0)),
        scratch_shapes=[pltpu.VMEM((MAX_PLANES * t_tile + PAIR_CHUNK, d), F32), pltpu.SemaphoreType.DMA],
    )
    return pl.pallas_call(
        _combine_kernel,
        grid_spec=grid_spec,
        out_shape=jax.ShapeDtypeStruct((nt, d), F32),
        compiler_params=_params("arbitrary"),
        name="moe_combine",
    )(pairs, off, kmax, ye, h, cnt_col)


def _combine_plan(idx_all, nt, t_tile):
    n_rows = idx_all.size
    n_tiles = nt // t_tile
    t_flat = idx_all.reshape(-1)
    order = jnp.argsort(t_flat).astype(jnp.int32)
    t_sorted = t_flat[order]
    cnt = jnp.zeros((nt,), jnp.int32).at[t_flat].add(1)
    start = jnp.cumsum(cnt) - cnt
    rank = jnp.arange(n_rows, dtype=jnp.int32)
    plane = rank - start[t_sorted]
    words = (order << PAIR_ROW_SHIFT) | (plane * t_tile + t_sorted % t_tile)
    tile_start = jnp.concatenate([start[::t_tile], jnp.full((1,), n_rows, jnp.int32)])
    chunks = (tile_start[1:] - tile_start[:-1] + PAIR_CHUNK - 1) // PAIR_CHUNK
    off = jnp.concatenate([jnp.zeros((1,), jnp.int32), jnp.cumsum(chunks).astype(jnp.int32)])
    tile_of = t_sorted // t_tile
    pos = off[tile_of] * PAIR_CHUNK + rank - tile_start[tile_of]
    n_words = n_rows + PAIR_CHUNK * n_tiles
    padding = MAX_PLANES * t_tile + jnp.arange(n_words, dtype=jnp.int32) % PAIR_CHUNK
    pairs = padding.at[pos].set(words)
    kmax = jnp.max(cnt.reshape(n_tiles, t_tile), axis=1)
    return pairs, off, kmax, cnt[:, None]


def _moe(h, tn, aff, group_sizes, w_gate, w_up, w_down):
    nt, _ = h.shape
    n_exp = aff.shape[1]
    idx_parts, g_parts, base = [], [], 0
    for n in group_sizes:
        cap = N_EXPERT_CAPACITY_FACTOR * n // n_exp
        g, idx = lax.top_k(aff[base: base + n].T, cap)
        idx_parts.append(idx.astype(jnp.int32) + base)
        g_parts.append(g)
        base += n
    idx_all = jnp.concatenate(idx_parts, axis=1)
    g_all = jnp.concatenate(g_parts, axis=1)
    slots = idx_all.shape[1]
    t_tile = _tile(nt, 160, 8)
    assert (n_exp * slots) < (1 << (31 - PAIR_ROW_SHIFT))
    assert MAX_PLANES * t_tile + PAIR_CHUNK <= (1 << PAIR_ROW_SHIFT)
    ye = _moe_ffn(idx_all.reshape(-1), tn, g_all.reshape(-1, 1), w_gate, w_up, w_down, slots)
    pairs, off, kmax, cnt_col = _combine_plan(idx_all, nt, t_tile)
    return _combine(pairs, off, kmax, ye, h, cnt_col, t_tile)


def _final_norm_kernel(x_ref, g_ref, o_ref):
    o_ref[...] = _rms(x_ref[...], g_ref[...])


def _final_norm(h, g):
    nt, d = h.shape
    tm = _tile(nt, 512, 8)
    return pl.pallas_call(
        _final_norm_kernel,
        grid=(nt // tm,),
        in_specs=[pl.BlockSpec((tm, d), lambda i: (i, 0)), pl.BlockSpec((1, d), lambda i: (0, 0))],
        out_specs=pl.BlockSpec((tm, d), lambda i: (i, 0)),
        out_shape=jax.ShapeDtypeStruct((nt, d), F32),
        compiler_params=_params("parallel"),
        name="final_norm",
    )(h, g)


def _lambda_init(layer):
    return 0.8 - 0.6 * math.exp(-0.3 * layer)


def kernel(x_prompt, x_sample, meta_tokens, mix_norm_g, ffn_norm_g, final_norm_g, ev_w_in, ev_w_out, ev_lambda_q1, ev_lambda_k1, ev_lambda_q2, ev_lambda_k2, ev_subln_g, ev_conv_w, od_w_in, od_q_norm_g, od_w_q_b, od_kv_norm_g, od_w_kv_b, od_w_out, moe_w_router, moe_w_gate, moe_w_up, moe_w_down):
    d = x_prompt.shape[-1]
    seq = N_META + x_prompt.shape[1]
    assert x_sample.shape[1] == x_prompt.shape[1]
    nb_p, nb_s = x_prompt.shape[0], x_sample.shape[0]
    nb = nb_p + nb_s
    nt = nb * seq
    group_sizes = (nb_p * seq, nb_s * seq)
    depth = mix_norm_g.shape[0]

    x = jnp.concatenate([x_prompt, x_sample], axis=0)
    meta = jnp.broadcast_to(meta_tokens[None].astype(x.dtype), (nb, N_META, d))
    h = jnp.concatenate([meta, x], axis=1).reshape(nt, d)
    cos128, sin128 = _rope_tables(seq)

    for layer in range(depth):
        i = layer // 2
        mix_g = mix_norm_g[layer][None]
        ffn_g = ffn_norm_g[layer][None]
        w_router = moe_w_router[layer].astype(BF16)
        if layer % 2 == 0:
            width = d // 2
            heads = width // LANES
            w_in = ev_w_in[i].astype(BF16)
            qk3 = _even_proj(h, mix_g, w_in, cos128, sin128, seq, True).reshape(nb, seq, 2 * width)
            rest3 = _even_proj(h, mix_g, w_in, cos128, sin128, seq, False).reshape(nb, seq, 4 * width)
            attn = _diff_attn(qk3, rest3, ev_lambda_q1[i][None], ev_lambda_k1[i][None], ev_lambda_q2[i][None],
                              ev_lambda_k2[i][None], ev_subln_g[i][None], _lambda_init(layer), heads)
            conv = _gated_conv(rest3, ev_conv_w[i], width)
            h, tn, aff = _out_proj_router(attn.reshape(nt, width), conv.reshape(nt, width), 0, 0,
                                          ev_w_out[i].astype(BF16), h, ffn_g, w_router)
        else:
            q_rank, kv_rank = od_q_norm_g.shape[1], od_kv_norm_g.shape[1]
            heads = od_w_q_b.shape[2] // (MLA_NOPE + MLA_ROPE)
            w_in_pad = jnp.pad(od_w_in[i], ((0, 0), (0, LANES - MLA_ROPE))).astype(BF16)
            w_q = od_w_q_b[i].reshape(q_rank, heads, MLA_NOPE + MLA_ROPE)
            w_q = jnp.pad(w_q, ((0, 0), (0, 0), (0, MLA_QK_PAD - MLA_NOPE - MLA_ROPE)))
            w_q = w_q.reshape(q_rank, heads * MLA_QK_PAD).astype(BF16)
            w_kv = od_w_kv_b[i].reshape(kv_rank, heads, MLA_NOPE + MLA_V)
            w_kv = jnp.concatenate([w_kv[:, :, :MLA_NOPE].reshape(kv_rank, heads * MLA_NOPE),
                                    w_kv[:, :, MLA_NOPE:].reshape(kv_rank, heads * MLA_V)], axis=1).astype(BF16)
            cq, ckv, kr = _odd_in(h, mix_g, w_in_pad, od_q_norm_g[i][None], od_kv_norm_g[i][None], cos128, sin128, seq)
            q = _up_proj(cq, w_q, cos128, sin128, seq, True, (MLA_NOPE + MLA_ROPE) ** -0.5)
            kv = _up_proj(ckv, w_kv, cos128, sin128, seq, False, 1.0)
            o = _mla_attn(q.reshape(nb, seq, heads * MLA_QK_PAD), kv.reshape(nb, seq, 2 * heads * LANES),
                          kr.reshape(nb, seq, LANES), heads)
            o = o.reshape(nt, heads * MLA_V)
            h, tn, aff = _out_proj_router(o, o, 0, 1, od_w_out[i].astype(BF16), h, ffn_g, w_router)
        h = _moe(h, tn, aff, group_sizes, moe_w_gate[layer].astype(BF16), moe_w_up[layer].astype(BF16),
                 moe_w_down[layer].astype(BF16))

    y = _final_norm(h, final_norm_g[None]).reshape(nb, seq, d)[:, N_META:]
    return (y[:nb_p], y[nb_p:])
```

```python
import functools
import math

import jax
import jax.numpy as jnp
from jax import lax
from jax.experimental import pallas as pl
from jax.experimental.pallas import tpu as pltpu

N_META = 16
EPS = 1e-6
ROPE_THETA = 10000.0
HEAD_DIM = 64
LANES = 128
MXU_COLS = 256
MLA_NOPE = 128
MLA_ROPE = 64
MLA_V = 128
MLA_QK_PAD = 256
ATTN_Q_TILE_MAX = 768
N_EXPERT_CAPACITY_FACTOR = 2
VMEM_LIMIT_BYTES = 56 * 1024 * 1024
NEG_BIG = -1e30
F32_INF_BITS = 0x7F800000
ROUTE_SLOT_BLOCK = 512
COMBINE_TOKENS_MAX = 192
COMBINE_PIECE = 8
COMBINE_CHUNK = 256

F32 = jnp.float32
BF16 = jnp.bfloat16


def _tile(n, cap, mult):
    best = None
    for t in range(mult, min(n, cap) + 1, mult):
        if n % t == 0:
            best = t
    assert best is not None, (n, cap, mult)
    return best


def _params(*sem):
    return pltpu.CompilerParams(dimension_semantics=sem, vmem_limit_bytes=VMEM_LIMIT_BYTES)


def _rms(x, g):
    return x * lax.rsqrt(jnp.mean(x * x, axis=-1, keepdims=True) + EPS) * g


def _rope128(x, cos, sin_signed):
    lane = lax.broadcasted_iota(jnp.int32, x.shape, 1)
    first_half = (lane % HEAD_DIM) < (HEAD_DIM // 2)
    rot = jnp.where(first_half, pltpu.roll(x, LANES - HEAD_DIM // 2, 1), pltpu.roll(x, HEAD_DIM // 2, 1))
    return x * cos + rot * sin_signed


def _rope_tables(length):
    inv = 1.0 / (ROPE_THETA ** (jnp.arange(0, HEAD_DIM, 2, dtype=F32) / HEAD_DIM))
    ang = jnp.arange(length, dtype=F32)[:, None] * inv[None, :]
    cos, sin = jnp.cos(ang), jnp.sin(ang)
    cos128 = jnp.tile(cos, (1, LANES // (HEAD_DIM // 2)))
    sin128 = jnp.tile(jnp.concatenate([-sin, sin], axis=1), (1, LANES // HEAD_DIM))
    return cos128, sin128


def _even_proj_kernel(x_ref, g_ref, w_ref, cos_ref, sin_ref, o_ref, xn_ref, *, rope, q_scale):
    j = pl.program_id(1)

    @pl.when(j == 0)
    def _():
        xn_ref[...] = _rms(x_ref[...], g_ref[...]).astype(BF16)

    xn = xn_ref[...]
    if rope:
        scale = jnp.where(j == 0, q_scale, 1.0).astype(F32)
        cos, sin = cos_ref[...], sin_ref[...]
    for c in range(o_ref.shape[1] // MXU_COLS):
        lo = c * MXU_COLS
        acc = jnp.dot(xn, w_ref[:, lo: lo + MXU_COLS], preferred_element_type=F32)
        if rope:
            for half in range(MXU_COLS // LANES):
                sl = slice(half * LANES, (half + 1) * LANES)
                o_ref[:, lo + half * LANES: lo + (half + 1) * LANES] = (
                    _rope128(acc[:, sl], cos, sin) * scale).astype(o_ref.dtype)
        else:
            o_ref[:, lo: lo + MXU_COLS] = acc.astype(o_ref.dtype)


def _even_proj(h, g, w_in, cos128, sin128, seq, rope):
    nt, d = h.shape
    tn = w_in.shape[1] // 6
    col0, n_cols = (0, 2) if rope else (2, 4)
    tm = _tile(seq, 768, 16)
    tiles_per_seq = seq // tm
    return pl.pallas_call(
        functools.partial(_even_proj_kernel, rope=rope, q_scale=HEAD_DIM ** -0.5),
        grid=(nt // tm, n_cols),
        in_specs=[
            pl.BlockSpec((tm, d), lambda i, j: (i, 0)),
            pl.BlockSpec((1, d), lambda i, j: (0, 0)),
            pl.BlockSpec((d, tn), lambda i, j: (0, col0 + j)),
            pl.BlockSpec((tm, LANES), lambda i, j: (i % tiles_per_seq, 0)),
            pl.BlockSpec((tm, LANES), lambda i, j: (i % tiles_per_seq, 0)),
        ],
        out_specs=pl.BlockSpec((tm, tn), lambda i, j: (i, j)),
        out_shape=jax.ShapeDtypeStruct((nt, n_cols * tn), BF16),
        scratch_shapes=[pltpu.VMEM((tm, d), BF16)],
        compiler_params=_params("parallel", "arbitrary"),
        name="even_proj_qk" if rope else "even_proj_rest",
    )(h, g, w_in, cos128, sin128)


def _store_scores(q, k_ref, tail_bias, s_ref):
    s = _nt_dot(q, k_ref[...])
    n_main = s.shape[1] - LANES
    s_ref[:, :n_main] = s[:, :n_main]
    s_ref[:, n_main:] = s[:, n_main:] + tail_bias


def _store_probs(s_ref, p_ref):
    s = s_ref[...]
    p = jnp.exp(s - jnp.max(s, axis=-1, keepdims=True))
    p_ref[...] = p.astype(p_ref.dtype)
    return jnp.sum(p, axis=-1, keepdims=True)


def _tail_bias(seq, lp):
    lane = lax.broadcasted_iota(jnp.int32, (1, LANES), 1)
    return jnp.where(lane < seq - (lp - LANES), 0.0, NEG_BIG).astype(F32)


def _nt_dot(a, b):
    return lax.dot_general(a, b, (((1,), (1,)), ((), ())), preferred_element_type=F32)


def _attention_pipeline(n_tiles, scores, softmax, pv):
    sums = {}
    for step in range(n_tiles + 2):
        if 0 <= step - 2 < n_tiles:
            pv(step - 2, (step - 2) % 2, sums.pop(step - 2))
        if 0 <= step - 1 < n_tiles:
            sums[step - 1] = softmax((step - 1) % 2)
        if step < n_tiles:
            scores(step, step % 2)


def _diff_attn_kernel(lq1_ref, lk1_ref, lq2_ref, lk2_ref, q_ref, k_ref, v_ref, g_ref, o_ref, kpad, vpad,
                      s1a, s1b, s2a, s2b, p1a, p1b, p2a, p2b, *, seq, lam_init, tq):
    lp = kpad.shape[0]
    kpad[0:seq, :] = k_ref[0]
    kpad[seq:lp, :] = jnp.zeros((lp - seq, LANES), BF16)
    vpad[0:seq, :] = v_ref[0]
    vpad[seq:lp, :] = jnp.zeros((lp - seq, LANES), BF16)
    lam = (jnp.exp(jnp.sum(lq1_ref[...] * lk1_ref[...], axis=-1, keepdims=True))
           - jnp.exp(jnp.sum(lq2_ref[...] * lk2_ref[...], axis=-1, keepdims=True)) + lam_init)
    tail_bias = _tail_bias(seq, lp)
    gain = g_ref[...] * (1.0 - lam_init)
    s_bufs, p_bufs = ((s1a, s2a), (s1b, s2b)), ((p1a, p2a), (p1b, p2b))

    def scores(t, par):
        q = q_ref[0, t * tq:(t + 1) * tq, :]
        lane = lax.broadcasted_iota(jnp.int32, q.shape, 1)
        zero = jnp.zeros_like(q)
        _store_scores(jnp.where(lane < HEAD_DIM, q, zero), kpad, tail_bias, s_bufs[par][0])
        _store_scores(jnp.where(lane >= HEAD_DIM, q, zero), kpad, tail_bias, s_bufs[par][1])

    def softmax(par):
        return [_store_probs(s_bufs[par][m], p_bufs[par][m]) for m in range(2)]

    def pv(t, par, l):
        o1 = jnp.dot(p_bufs[par][0][...], vpad[...], preferred_element_type=F32) / l[0]
        o2 = jnp.dot(p_bufs[par][1][...], vpad[...], preferred_element_type=F32) / l[1]
        o = o1 - lam * o2
        o = o * lax.rsqrt(jnp.mean(o * o, axis=-1, keepdims=True) + EPS) * gain
        o_ref[0, t * tq:(t + 1) * tq, :] = o.astype(o_ref.dtype)

    _attention_pipeline(seq // tq, scores, softmax, pv)


def _diff_attn(qk3, rest3, lq1, lk1, lq2, lk2, subln_g, lam_init, heads):
    nb, seq, _ = qk3.shape
    lp = pl.cdiv(seq, LANES) * LANES
    assert lp > seq
    tq = _tile(seq, ATTN_Q_TILE_MAX, 16)
    vec = lambda: pl.BlockSpec((1, HEAD_DIM), lambda b, h: (0, 0))
    return pl.pallas_call(
        functools.partial(_diff_attn_kernel, seq=seq, lam_init=lam_init, tq=tq),
        grid=(nb, heads),
        in_specs=[
            vec(), vec(), vec(), vec(),
            pl.BlockSpec((1, seq, LANES), lambda b, h: (b, 0, h)),
            pl.BlockSpec((1, seq, LANES), lambda b, h: (b, 0, heads + h)),
            pl.BlockSpec((1, seq, LANES), lambda b, h: (b, 0, h)),
            pl.BlockSpec((1, LANES), lambda b, h: (0, 0)),
        ],
        out_specs=pl.BlockSpec((1, seq, LANES), lambda b, h: (b, 0, h)),
        out_shape=jax.ShapeDtypeStruct((nb, seq, heads * LANES), BF16),
        scratch_shapes=[pltpu.VMEM((lp, LANES), BF16), pltpu.VMEM((lp, LANES), BF16)]
        + [pltpu.VMEM((tq, lp), F32)] * 4 + [pltpu.VMEM((tq, lp), BF16)] * 4,
        compiler_params=_params("parallel", "parallel"),
        name="diff_attn",
    )(lq1, lk1, lq2, lk2, qk3, qk3, rest3, subln_g)


def _conv_kernel(gb_ref, gc_ref, cx_ref, w_ref, o_ref):
    u = gc_ref[0].astype(F32) * cx_ref[0].astype(F32)
    seq = u.shape[0]
    row = lax.broadcasted_iota(jnp.int32, u.shape, 0)
    prev = jnp.where(row == 0, 0.0, pltpu.roll(u, 1, 0))
    nxt = jnp.where(row == seq - 1, 0.0, pltpu.roll(u, seq - 1, 0))
    y = prev * w_ref[0:1, :] + u * w_ref[1:2, :] + nxt * w_ref[2:3, :]
    o_ref[0] = (gb_ref[0].astype(F32) * y).astype(o_ref.dtype)


def _gated_conv(rest3, conv_w, width):
    nb, seq, _ = rest3.shape
    tc = _tile(width, 256, LANES)
    nc = width // tc
    blk = lambda part: pl.BlockSpec((1, seq, tc), lambda b, c: (b, 0, part * nc + c))
    return pl.pallas_call(
        _conv_kernel,
        grid=(nb, nc),
        in_specs=[blk(1), blk(2), blk(3), pl.BlockSpec((3, tc), lambda b, c: (0, c))],
        out_specs=pl.BlockSpec((1, seq, tc), lambda b, c: (b, 0, c)),
        out_shape=jax.ShapeDtypeStruct((nb, seq, width), BF16),
        compiler_params=_params("parallel", "parallel"),
        name="gated_conv",
    )(rest3, rest3, rest3, conv_w)


def _out_proj_router_kernel(a1_ref, a2_ref, w_ref, h_ref, g_ref, wr_ref, hn_ref, tn_ref, aff_ref):
    k1 = a1_ref.shape[1]
    y = jnp.dot(a1_ref[...], w_ref[0:k1, :], preferred_element_type=F32)
    y = y + jnp.dot(a2_ref[...], w_ref[k1:, :], preferred_element_type=F32)
    h = h_ref[...] + y
    hn_ref[...] = h
    t = _rms(h, g_ref[...])
    d = t.shape[1]
    n_exp = aff_ref.shape[1]
    logits = jnp.dot(t.astype(BF16), wr_ref[...], preferred_element_type=F32)
    lane = lax.broadcasted_iota(jnp.int32, logits.shape, 1)
    logits = jnp.where(lane < n_exp, logits, NEG_BIG)
    e = jnp.exp(logits - jnp.max(logits, axis=-1, keepdims=True))
    aff = e / jnp.sum(e, axis=-1, keepdims=True)
    aff_ref[...] = aff[:, :n_exp]
    tn_ref[:, :d] = t
    tn_ref[:, d:] = aff


def _out_proj_router(a1, a2, col1, col2, w_out, h, ffn_g, w_router):
    nt, d = h.shape
    k1 = w_out.shape[0] // 2
    n_exp = w_router.shape[1]
    w_router = jnp.pad(w_router, ((0, 0), (0, LANES - n_exp)))
    tm = _tile(nt, 512, 16)
    row = lambda c: (lambda i: (i, c))
    const = lambda i: (0, 0)
    return pl.pallas_call(
        _out_proj_router_kernel,
        grid=(nt // tm,),
        in_specs=[
            pl.BlockSpec((tm, k1), row(col1)),
            pl.BlockSpec((tm, k1), row(col2)),
            pl.BlockSpec(w_out.shape, const),
            pl.BlockSpec((tm, d), row(0)),
            pl.BlockSpec((1, d), const),
            pl.BlockSpec(w_router.shape, const),
        ],
        out_specs=[
            pl.BlockSpec((tm, d), row(0)),
            pl.BlockSpec((tm, d + LANES), row(0)),
            pl.BlockSpec((tm, n_exp), row(0)),
        ],
        out_shape=[
            jax.ShapeDtypeStruct((nt, d), F32),
            jax.ShapeDtypeStruct((nt, d + LANES), F32),
            jax.ShapeDtypeStruct((nt, n_exp), F32),
        ],
        compiler_params=_params("parallel"),
        name="out_proj_router",
    )(a1, a2, w_out, h, ffn_g, w_router)


def _odd_in_kernel(x_ref, g_ref, w_ref, qg_ref, kvg_ref, cos_ref, sin_ref, cq_ref, ckv_ref, kr_ref):
    xn = _rms(x_ref[...], g_ref[...]).astype(BF16)
    p = jnp.dot(xn, w_ref[...], preferred_element_type=F32)
    qr, kvr = cq_ref.shape[1], ckv_ref.shape[1]
    cq_ref[...] = _rms(p[:, :qr], qg_ref[...]).astype(cq_ref.dtype)
    ckv_ref[...] = _rms(p[:, qr: qr + kvr], kvg_ref[...]).astype(ckv_ref.dtype)
    kr_ref[...] = _rope128(p[:, qr + kvr:], cos_ref[...], sin_ref[...]).astype(kr_ref.dtype)


def _odd_in(h, g, w_in_pad, q_g, kv_g, cos128, sin128, seq):
    nt, d = h.shape
    qr, kvr = q_g.shape[1], kv_g.shape[1]
    tm = _tile(seq, 768, 16)
    tiles_per_seq = seq // tm
    row = lambda i: (i, 0)
    const = lambda i: (0, 0)
    pos = lambda i: (i % tiles_per_seq, 0)
    return pl.pallas_call(
        _odd_in_kernel,
        grid=(nt // tm,),
        in_specs=[
            pl.BlockSpec((tm, d), row),
            pl.BlockSpec((1, d), const),
            pl.BlockSpec(w_in_pad.shape, const),
            pl.BlockSpec((1, qr), const),
            pl.BlockSpec((1, kvr), const),
            pl.BlockSpec((tm, LANES), pos),
            pl.BlockSpec((tm, LANES), pos),
        ],
        out_specs=[
            pl.BlockSpec((tm, qr), row),
            pl.BlockSpec((tm, kvr), row),
            pl.BlockSpec((tm, LANES), row),
        ],
        out_shape=[
            jax.ShapeDtypeStruct((nt, qr), BF16),
            jax.ShapeDtypeStruct((nt, kvr), BF16),
            jax.ShapeDtypeStruct((nt, LANES), BF16),
        ],
        compiler_params=_params("parallel"),
        name="odd_in",
    )(h, g, w_in_pad, q_g, kv_g, cos128, sin128)


def _up_proj_kernel(a_ref, w_ref, cos_ref, sin_ref, o_ref, *, rope, scale):
    a = a_ref[...]
    cos, sin = cos_ref[...], sin_ref[...]
    for c in range(o_ref.shape[1] // MLA_QK_PAD):
        lo = c * MLA_QK_PAD
        acc = jnp.dot(a, w_ref[:, lo: lo + MLA_QK_PAD], preferred_element_type=F32)
        if rope:
            o_ref[:, lo: lo + LANES] = (acc[:, :LANES] * scale).astype(o_ref.dtype)
            o_ref[:, lo + LANES: lo + MLA_QK_PAD] = (_rope128(acc[:, LANES:], cos, sin) * scale).astype(o_ref.dtype)
        else:
            o_ref[:, lo: lo + MLA_QK_PAD] = acc.astype(o_ref.dtype)


def _up_proj(a, w, cos128, sin128, seq, rope, scale):
    nt, k = a.shape
    n = w.shape[1]
    tm = _tile(seq, 768, 16)
    tiles_per_seq = seq // tm
    pos = lambda i: (i % tiles_per_seq, 0)
    return pl.pallas_call(
        functools.partial(_up_proj_kernel, rope=rope, scale=scale),
        grid=(nt // tm,),
        in_specs=[
            pl.BlockSpec((tm, k), lambda i: (i, 0)),
            pl.BlockSpec((k, n), lambda i: (0, 0)),
            pl.BlockSpec((tm, LANES), pos),
            pl.BlockSpec((tm, LANES), pos),
        ],
        out_specs=pl.BlockSpec((tm, n), lambda i: (i, 0)),
        out_shape=jax.ShapeDtypeStruct((nt, n), BF16),
        compiler_params=_params("parallel"),
        name="q_up_proj" if rope else "kv_up_proj",
    )(a, w, cos128, sin128)


def _mla_attn_kernel(q_ref, kn_ref, kr_ref, v_ref, o_ref, kcat, vpad, sa, sb, pa, pb, *, seq, tq):
    lp = kcat.shape[0]
    kcat[0:seq, 0:LANES] = kn_ref[0]
    kcat[0:seq, LANES:] = kr_ref[0]
    kcat[seq:lp, :] = jnp.zeros((lp - seq, MLA_QK_PAD), BF16)
    vpad[0:seq, :] = v_ref[0]
    vpad[seq:lp, :] = jnp.zeros((lp - seq, LANES), BF16)
    tail_bias = _tail_bias(seq, lp)

    s_bufs, p_bufs = (sa, sb), (pa, pb)

    def scores(t, par):
        _store_scores(q_ref[0, t * tq:(t + 1) * tq, :], kcat, tail_bias, s_bufs[par])

    def softmax(par):
        return _store_probs(s_bufs[par], p_bufs[par])

    def pv(t, par, l):
        o = jnp.dot(p_bufs[par][...], vpad[...], preferred_element_type=F32) / l
        o_ref[0, t * tq:(t + 1) * tq, :] = o.astype(o_ref.dtype)

    _attention_pipeline(seq // tq, scores, softmax, pv)


def _mla_attn(q3, kv3, kr3, heads):
    nb, seq, _ = q3.shape
    lp = pl.cdiv(seq, LANES) * LANES
    assert lp > seq
    tq = _tile(seq, ATTN_Q_TILE_MAX, 16)
    return pl.pallas_call(
        functools.partial(_mla_attn_kernel, seq=seq, tq=tq),
        grid=(nb, heads),
        in_specs=[
            pl.BlockSpec((1, seq, MLA_QK_PAD), lambda b, h: (b, 0, h)),
            pl.BlockSpec((1, seq, LANES), lambda b, h: (b, 0, h)),
            pl.BlockSpec((1, seq, LANES), lambda b, h: (b, 0, 0)),
            pl.BlockSpec((1, seq, LANES), lambda b, h: (b, 0, heads + h)),
        ],
        out_specs=pl.BlockSpec((1, seq, LANES), lambda b, h: (b, 0, h)),
        out_shape=jax.ShapeDtypeStruct((nb, seq, heads * LANES), BF16),
        scratch_shapes=[pltpu.VMEM((lp, MLA_QK_PAD), BF16), pltpu.VMEM((lp, LANES), BF16)]
        + [pltpu.VMEM((tq, lp), F32)] * 2 + [pltpu.VMEM((tq, lp), BF16)] * 2,
        compiler_params=_params("parallel", "parallel"),
        name="mla_attn",
    )(q3, kv3, kr3, kv3)


def _moe_ffn_kernel(idx_ref, tn_hbm, wg_ref, wu_ref, wd_ref, o_ref, xs_buf, sem, *, tm):
    step = pl.program_id(0) * pl.num_programs(1) + pl.program_id(1)
    n_steps = pl.num_programs(0) * pl.num_programs(1)
    slot = step % 2

    def gather(tile, into):
        for r in range(tm):
            tok = idx_ref[tile * tm + r]
            pltpu.make_async_copy(tn_hbm.at[pl.ds(tok, 1), :], xs_buf.at[into, pl.ds(r, 1), :], sem.at[into]).start()

    def wait(which):
        pltpu.make_async_copy(tn_hbm.at[pl.ds(0, tm), :], xs_buf.at[which], sem.at[which]).wait()

    @pl.when(step == 0)
    def _():
        gather(0, 0)

    wait(slot)
    d = o_ref.shape[1]
    xs = xs_buf[slot, :, :d].astype(BF16)
    aff = xs_buf[slot, :, d:]
    lane = lax.broadcasted_iota(jnp.int32, aff.shape, 1)
    gate = jnp.sum(jnp.where(lane == pl.program_id(0), aff, 0.0), axis=-1, keepdims=True)
    hg = jnp.dot(xs, wg_ref[...], preferred_element_type=F32)
    hu = jnp.dot(xs, wu_ref[...], preferred_element_type=F32)
    hid = (hg * jax.nn.sigmoid(hg) * hu).astype(BF16)
    gather(step + 1, 1 - slot)
    o_ref[...] = jnp.dot(hid, wd_ref[...], preferred_element_type=F32) * gate

    @pl.when(step == n_steps - 1)
    def _():
        wait(1 - slot)


def _moe_ffn(idx_flat, tn, w_gate, w_up, w_down, slots):
    n_exp, d, ff = w_gate.shape
    tm = _tile(slots, 384, 8)
    tiles = slots // tm
    grid_spec = pltpu.PrefetchScalarGridSpec(
        num_scalar_prefetch=1,
        grid=(n_exp, tiles),
        in_specs=[
            pl.BlockSpec(memory_space=pl.ANY),
            pl.BlockSpec((None, d, ff), lambda e, m, idx: (e, 0, 0)),
            pl.BlockSpec((None, d, ff), lambda e, m, idx: (e, 0, 0)),
            pl.BlockSpec((None, ff, d), lambda e, m, idx: (e, 0, 0)),
        ],
        out_specs=pl.BlockSpec((tm, d), lambda e, m, idx: (e * tiles + m, 0)),
        scratch_shapes=[pltpu.VMEM((2, tm, tn.shape[1]), F32), pltpu.SemaphoreType.DMA((2,))],
    )
    idx_padded = jnp.concatenate([idx_flat, idx_flat[:tm]])
    return pl.pallas_call(
        functools.partial(_moe_ffn_kernel, tm=tm),
        grid_spec=grid_spec,
        out_shape=jax.ShapeDtypeStruct((n_exp * slots, d), F32),
        compiler_params=_params("arbitrary", "arbitrary"),
        name="moe_ffn",
    )(idx_padded, tn, w_gate, w_up, w_down)


def _combine_kernel(src_ref, pieces_ref, dst_ref, chunks_ref, ye_hbm, h_ref, tgt_ref, o_ref, stage, sem):
    i = pl.program_id(0)
    n_exp = tgt_ref.shape[1]

    @pl.when(i == 0)
    def _():
        stage[...] = jnp.zeros(stage.shape, F32)

    def piece(src_row, dst_row):
        return pltpu.make_async_copy(ye_hbm.at[pl.ds(src_row, COMBINE_PIECE), :],
                                     stage.at[pl.ds(dst_row, COMBINE_PIECE), :], sem)

    n_pieces = 0
    for e in range(n_exp):
        src, dst, n = src_ref[i * n_exp + e], dst_ref[i * n_exp + e], pieces_ref[i * n_exp + e]

        def issue(p, carry, src=src, dst=dst):
            piece(pl.multiple_of(src + p * COMBINE_PIECE, COMBINE_PIECE),
                  pl.multiple_of(dst + p * COMBINE_PIECE, COMBINE_PIECE)).start()
            return carry

        lax.fori_loop(0, n, issue, 0)
        n_pieces = n_pieces + n

    def wait(p, carry):
        piece(0, 0).wait()
        return carry

    lax.fori_loop(0, n_pieces, wait, 0)
    o_ref[...] = h_ref[...]
    tgt = tgt_ref[...]
    lane = lax.broadcasted_iota(jnp.int32, (1, COMBINE_CHUNK), 1)

    def chunk(c, carry):
        base = c * COMBINE_CHUNK
        onehot = jnp.zeros((tgt.shape[0], COMBINE_CHUNK), F32)
        for e in range(n_exp):
            onehot = jnp.where(tgt[:, e:e + 1] - base == lane, 1.0, onehot)
        onehot = onehot.astype(BF16)
        rows = stage[pl.ds(pl.multiple_of(base, COMBINE_CHUNK), COMBINE_CHUNK), :]
        hi = rows.astype(BF16)
        lo = (rows - hi.astype(F32)).astype(BF16)
        o_ref[...] += (jnp.dot(onehot, hi, preferred_element_type=F32)
                       + jnp.dot(onehot, lo, preferred_element_type=F32))
        return carry

    lax.fori_loop(0, chunks_ref[i], chunk, 0)


def _combine(ye, h, sel_t, row_t, t_tile):
    nt, d = h.shape
    n_exp = sel_t.shape[1]
    assert ye.shape[0] % COMBINE_PIECE == 0
    first = row_t[::t_tile]
    last = (row_t + sel_t)[t_tile - 1::t_tile]
    src = first // COMBINE_PIECE * COMBINE_PIECE
    pieces = (last - src + COMBINE_PIECE - 1) // COMBINE_PIECE
    dst = (jnp.cumsum(pieces, axis=1) - pieces) * COMBINE_PIECE
    chunks = (jnp.sum(pieces, axis=1) * COMBINE_PIECE + COMBINE_CHUNK - 1) // COMBINE_CHUNK
    tgt = jnp.where(sel_t > 0, row_t - jnp.repeat(src - dst, t_tile, axis=0), -1)
    stage_rows = pl.cdiv(n_exp * (t_tile + 2 * (COMBINE_PIECE - 1)), COMBINE_CHUNK) * COMBINE_CHUNK
    grid_spec = pltpu.PrefetchScalarGridSpec(
        num_scalar_prefetch=4,
        grid=(nt // t_tile,),
        in_specs=[
            pl.BlockSpec(memory_space=pl.ANY),
            pl.BlockSpec((t_tile, d), lambda i, *_: (i, 0)),
            pl.BlockSpec((t_tile, n_exp), lambda i, *_: (i, 0)),
        ],
        out_specs=pl.BlockSpec((t_tile, d), lambda i, *_: (i, 0)),
        scratch_shapes=[pltpu.VMEM((stage_rows, d), F32), pltpu.SemaphoreType.DMA],
    )
    flat = lambda x: x.reshape(-1).astype(jnp.int32)
    return pl.pallas_call(
        _combine_kernel,
        grid_spec=grid_spec,
        out_shape=jax.ShapeDtypeStruct((nt, d), F32),
        compiler_params=_params("arbitrary"),
        name="moe_combine",
    )(flat(src), flat(pieces), flat(dst), flat(chunks), ye, h, tgt.astype(jnp.int32))


def _route_kernel(x_ref, sel_ref, pexc_ref, idx_ref, *, cap, slot_block):
    x = x_ref[0]
    n_rows = x.shape[0]
    bits = pltpu.bitcast(x, jnp.int32)
    cap_f = float(cap)

    def count(mask):
        return jnp.sum(jnp.where(mask, 1.0, 0.0), axis=(0, 1), keepdims=True)

    def bisect(_, bounds):
        lo, hi = bounds
        mid = lo + ((hi - lo) >> 1)
        ok = count(bits >= mid) >= cap_f
        return jnp.where(ok, mid, lo), jnp.where(ok, hi, mid)

    tau, _ = lax.fori_loop(0, 31, bisect, (jnp.zeros((1, 1), jnp.int32), jnp.full((1, 1), F32_INF_BITS, jnp.int32)))

    upper = jnp.where(lax.broadcasted_iota(jnp.int32, (LANES, LANES), 0)
                      <= lax.broadcasted_iota(jnp.int32, (LANES, LANES), 1), 1.0, 0.0).astype(BF16)
    lower = jnp.where(lax.broadcasted_iota(jnp.int32, (n_rows, n_rows), 1)
                      < lax.broadcasted_iota(jnp.int32, (n_rows, n_rows), 0), 1.0, 0.0).astype(BF16)

    def prefix(m):
        within = jnp.dot(m.astype(BF16), upper, preferred_element_type=F32)
        row_tot = jnp.broadcast_to(within[:, LANES - 1:], within.shape)
        before = jnp.dot(lower, row_tot.astype(BF16), preferred_element_type=F32)
        return within, before, row_tot

    gt = jnp.where(bits > tau, 1.0, 0.0)
    eq = jnp.where(bits == tau, 1.0, 0.0)
    need = cap_f - jnp.sum(gt, axis=(0, 1), keepdims=True)
    tie_within, tie_before, _ = prefix(eq)
    tie_rank = tie_within + tie_before - eq
    sel = gt + eq * jnp.where(tie_rank < need, 1.0, 0.0)
    within, before, row_tot = prefix(sel)
    sel_ref[0] = sel.astype(jnp.int32)
    pexc_ref[0] = (within + before - sel).astype(jnp.int32)

    row_end = jnp.transpose(before + row_tot)[0:1, :]
    row_start = jnp.transpose(before)[0:1, :]
    within_b = within.astype(BF16)
    row_id = lax.broadcasted_iota(jnp.int32, (1, n_rows), 1).astype(F32)
    for s in range(idx_ref.shape[1] // slot_block):
        j = (lax.broadcasted_iota(jnp.int32, (slot_block, 1), 0) + s * slot_block).astype(F32)
        row = jnp.sum(jnp.where(row_end <= j, 1.0, 0.0), axis=-1, keepdims=True)
        onehot = jnp.where(row_id == row, 1.0, 0.0)
        cum = jnp.dot(onehot.astype(BF16), within_b, preferred_element_type=F32)
        local = j - jnp.sum(onehot * row_start, axis=-1, keepdims=True)
        col = jnp.sum(jnp.where(cum <= local, 1.0, 0.0), axis=-1, keepdims=True)
        tok = (row * LANES + col).astype(jnp.int32)
        idx_ref[0, s * slot_block:(s + 1) * slot_block, :] = jnp.where(j < cap_f, tok, 0)


def _route(aff_rows, cap):
    n_exp, n_rows, _ = aff_rows.shape
    cap_pad = pl.cdiv(cap, ROUTE_SLOT_BLOCK) * ROUTE_SLOT_BLOCK
    blk = pl.BlockSpec((1, n_rows, LANES), lambda e: (e, 0, 0))
    return pl.pallas_call(
        functools.partial(_route_kernel, cap=cap, slot_block=ROUTE_SLOT_BLOCK),
        grid=(n_exp,),
        in_specs=[blk],
        out_specs=[blk, blk, pl.BlockSpec((1, cap_pad, 1), lambda e: (e, 0, 0))],
        out_shape=[
            jax.ShapeDtypeStruct(aff_rows.shape, jnp.int32),
            jax.ShapeDtypeStruct(aff_rows.shape, jnp.int32),
            jax.ShapeDtypeStruct((n_exp, cap_pad, 1), jnp.int32),
        ],
        compiler_params=_params("parallel"),
        name="route",
    )(aff_rows)


def _moe(h, tn, aff, group_sizes, w_gate, w_up, w_down):
    n_exp = aff.shape[1]
    caps = [N_EXPERT_CAPACITY_FACTOR * n // n_exp for n in group_sizes]
    slots = sum(caps)
    idx_parts, sel_parts, row_parts = [], [], []
    tok0, slot0 = 0, 0
    for n, cap in zip(group_sizes, caps):
        n_rows = pl.cdiv(pl.cdiv(n, LANES), LANES) * LANES
        rows = jnp.pad(aff[tok0: tok0 + n].T, ((0, 0), (0, n_rows * LANES - n)), constant_values=-1.0)
        sel, pexc, idx = _route(rows.reshape(n_exp, n_rows, LANES), cap)
        idx_parts.append(idx[:, :cap, 0] + tok0)
        sel_parts.append(sel.reshape(n_exp, -1)[:, :n].T)
        expert_row0 = jnp.arange(n_exp, dtype=jnp.int32)[None, :] * slots + slot0
        row_parts.append(pexc.reshape(n_exp, -1)[:, :n].T + expert_row0)
        tok0 += n
        slot0 += cap
    idx_all = jnp.concatenate(idx_parts, axis=1)
    sel_t = jnp.concatenate(sel_parts, axis=0)
    row_t = jnp.concatenate(row_parts, axis=0)
    ye = _moe_ffn(idx_all.reshape(-1), tn, w_gate, w_up, w_down, slots)
    t_tile = _tile(math.gcd(*group_sizes), COMBINE_TOKENS_MAX, 8)
    return _combine(ye, h, sel_t, row_t, t_tile)


def _final_norm_kernel(x_ref, g_ref, o_ref):
    o_ref[...] = _rms(x_ref[...], g_ref[...])


def _final_norm(h, g):
    nt, d = h.shape
    tm = _tile(nt, 512, 8)
    return pl.pallas_call(
        _final_norm_kernel,
        grid=(nt // tm,),
        in_specs=[pl.BlockSpec((tm, d), lambda i: (i, 0)), pl.BlockSpec((1, d), lambda i: (0, 0))],
        out_specs=pl.BlockSpec((tm, d), lambda i: (i, 0)),
        out_shape=jax.ShapeDtypeStruct((nt, d), F32),
        compiler_params=_params("parallel"),
        name="final_norm",
    )(h, g)


def _lambda_init(layer):
    return 0.8 - 0.6 * math.exp(-0.3 * layer)


def kernel(x_prompt, x_sample, meta_tokens, mix_norm_g, ffn_norm_g, final_norm_g, ev_w_in, ev_w_out, ev_lambda_q1, ev_lambda_k1, ev_lambda_q2, ev_lambda_k2, ev_subln_g, ev_conv_w, od_w_in, od_q_norm_g, od_w_q_b, od_kv_norm_g, od_w_kv_b, od_w_out, moe_w_router, moe_w_gate, moe_w_up, moe_w_down):
    d = x_prompt.shape[-1]
    seq = N_META + x_prompt.shape[1]
    assert x_sample.shape[1] == x_prompt.shape[1]
    nb_p, nb_s = x_prompt.shape[0], x_sample.shape[0]
    nb = nb_p + nb_s
    nt = nb * seq
    group_sizes = (nb_p * seq, nb_s * seq)
    depth = mix_norm_g.shape[0]

    x = jnp.concatenate([x_prompt, x_sample], axis=0)
    meta = jnp.broadcast_to(meta_tokens[None].astype(x.dtype), (nb, N_META, d))
    h = jnp.concatenate([meta, x], axis=1).reshape(nt, d)
    cos128, sin128 = _rope_tables(seq)

    for layer in range(depth):
        i = layer // 2
        mix_g = mix_norm_g[layer][None]
        ffn_g = ffn_norm_g[layer][None]
        w_router = moe_w_router[layer].astype(BF16)
        if layer % 2 == 0:
            width = d // 2
            heads = width // LANES
            w_in = ev_w_in[i].astype(BF16)
            qk3 = _even_proj(h, mix_g, w_in, cos128, sin128, seq, True).reshape(nb, seq, 2 * width)
            rest3 = _even_proj(h, mix_g, w_in, cos128, sin128, seq, False).reshape(nb, seq, 4 * width)
            attn = _diff_attn(qk3, rest3, ev_lambda_q1[i][None], ev_lambda_k1[i][None], ev_lambda_q2[i][None],
                              ev_lambda_k2[i][None], ev_subln_g[i][None], _lambda_init(layer), heads)
            conv = _gated_conv(rest3, ev_conv_w[i], width)
            h, tn, aff = _out_proj_router(attn.reshape(nt, width), conv.reshape(nt, width), 0, 0,
                                          ev_w_out[i].astype(BF16), h, ffn_g, w_router)
        else:
            q_rank, kv_rank = od_q_norm_g.shape[1], od_kv_norm_g.shape[1]
            heads = od_w_q_b.shape[2] // (MLA_NOPE + MLA_ROPE)
            w_in_pad = jnp.pad(od_w_in[i], ((0, 0), (0, LANES - MLA_ROPE))).astype(BF16)
            w_q = od_w_q_b[i].reshape(q_rank, heads, MLA_NOPE + MLA_ROPE)
            w_q = jnp.pad(w_q, ((0, 0), (0, 0), (0, MLA_QK_PAD - MLA_NOPE - MLA_ROPE)))
            w_q = w_q.reshape(q_rank, heads * MLA_QK_PAD).astype(BF16)
            w_kv = od_w_kv_b[i].reshape(kv_rank, heads, MLA_NOPE + MLA_V)
            w_kv = jnp.concatenate([w_kv[:, :, :MLA_NOPE].reshape(kv_rank, heads * MLA_NOPE),
                                    w_kv[:, :, MLA_NOPE:].reshape(kv_rank, heads * MLA_V)], axis=1).astype(BF16)
            cq, ckv, kr = _odd_in(h, mix_g, w_in_pad, od_q_norm_g[i][None], od_kv_norm_g[i][None], cos128, sin128, seq)
            q = _up_proj(cq, w_q, cos128, sin128, seq, True, (MLA_NOPE + MLA_ROPE) ** -0.5)
            kv = _up_proj(ckv, w_kv, cos128, sin128, seq, False, 1.0)
            o = _mla_attn(q.reshape(nb, seq, heads * MLA_QK_PAD), kv.reshape(nb, seq, 2 * heads * LANES),
                          kr.reshape(nb, seq, LANES), heads)
            o = o.reshape(nt, heads * MLA_V)
            h, tn, aff = _out_proj_router(o, o, 0, 1, od_w_out[i].astype(BF16), h, ffn_g, w_router)
        h = _moe(h, tn, aff, group_sizes, moe_w_gate[layer].astype(BF16), moe_w_up[layer].astype(BF16),
                 moe_w_down[layer].astype(BF16))

    y = _final_norm(h, final_norm_g[None]).reshape(nb, seq, d)[:, N_META:]
    return (y[:nb_p], y[nb_p:])
```

```python
import functools
import math

import jax
import jax.numpy as jnp
from jax import lax
from jax.experimental import pallas as pl
from jax.experimental.pallas import tpu as pltpu

N_META = 16
EPS = 1e-6
ROPE_THETA = 10000.0
HEAD_DIM = 64
LANES = 128
MXU_COLS = 256
MLA_NOPE = 128
MLA_ROPE = 64
MLA_V = 128
MLA_QK_PAD = 256
ATTN_Q_TILE_MAX = 768
N_EXPERT_CAPACITY_FACTOR = 2
VMEM_LIMIT_BYTES = 56 * 1024 * 1024
NEG_BIG = -1e30
F32_INF_BITS = 0x7F800000
LOG2E = math.log2(math.e)
ROUTE_SLOT_BLOCK = 512
COMBINE_TOKENS_MAX = 192
COMBINE_PIECE = 16
FFN_ROWS_MAX = 448
FFN_SLOT_ALIGN = 64
COMBINE_CHUNK = 256

F32 = jnp.float32
BF16 = jnp.bfloat16


def _tile(n, cap, mult):
    best = None
    for t in range(mult, min(n, cap) + 1, mult):
        if n % t == 0:
            best = t
    assert best is not None, (n, cap, mult)
    return best


def _params(*sem):
    return pltpu.CompilerParams(dimension_semantics=sem, vmem_limit_bytes=VMEM_LIMIT_BYTES)


def _rms(x, g):
    return x * lax.rsqrt(jnp.mean(x * x, axis=-1, keepdims=True) + EPS) * g


def _rope128(x, cos, sin_signed):
    lane = lax.broadcasted_iota(jnp.int32, x.shape, 1)
    first_half = (lane % HEAD_DIM) < (HEAD_DIM // 2)
    rot = jnp.where(first_half, pltpu.roll(x, LANES - HEAD_DIM // 2, 1), pltpu.roll(x, HEAD_DIM // 2, 1))
    return x * cos + rot * sin_signed


def _rope_tables(length):
    inv = 1.0 / (ROPE_THETA ** (jnp.arange(0, HEAD_DIM, 2, dtype=F32) / HEAD_DIM))
    ang = jnp.arange(length, dtype=F32)[:, None] * inv[None, :]
    cos, sin = jnp.cos(ang), jnp.sin(ang)
    cos128 = jnp.tile(cos, (1, LANES // (HEAD_DIM // 2)))
    sin128 = jnp.tile(jnp.concatenate([-sin, sin], axis=1), (1, LANES // HEAD_DIM))
    return cos128, sin128


def _even_proj_kernel(x_ref, g_ref, w_ref, cos_ref, sin_ref, o_ref, xn_ref, *, rope, q_scale):
    j = pl.program_id(1)

    @pl.when(j == 0)
    def _():
        xn_ref[...] = _rms(x_ref[...], g_ref[...]).astype(BF16)

    xn = xn_ref[...]
    if rope:
        scale = jnp.where(j == 0, q_scale, 1.0).astype(F32)
        cos, sin = cos_ref[...], sin_ref[...]
    for c in range(o_ref.shape[1] // MXU_COLS):
        lo = c * MXU_COLS
        acc = jnp.dot(xn, w_ref[:, lo: lo + MXU_COLS], preferred_element_type=F32)
        if rope:
            for half in range(MXU_COLS // LANES):
                sl = slice(half * LANES, (half + 1) * LANES)
                o_ref[:, lo + half * LANES: lo + (half + 1) * LANES] = (
                    _rope128(acc[:, sl], cos, sin) * scale).astype(o_ref.dtype)
        else:
            o_ref[:, lo: lo + MXU_COLS] = acc.astype(o_ref.dtype)


def _even_proj(h, g, w_in, cos128, sin128, seq, rope):
    nt, d = h.shape
    tn = w_in.shape[1] // 6
    col0, n_cols = (0, 2) if rope else (2, 4)
    tm = _tile(seq, 768, 16)
    tiles_per_seq = seq // tm
    return pl.pallas_call(
        functools.partial(_even_proj_kernel, rope=rope, q_scale=HEAD_DIM ** -0.5 * LOG2E),
        grid=(nt // tm, n_cols),
        in_specs=[
            pl.BlockSpec((tm, d), lambda i, j: (i, 0)),
            pl.BlockSpec((1, d), lambda i, j: (0, 0)),
            pl.BlockSpec((d, tn), lambda i, j: (0, col0 + j)),
            pl.BlockSpec((tm, LANES), lambda i, j: (i % tiles_per_seq, 0)),
            pl.BlockSpec((tm, LANES), lambda i, j: (i % tiles_per_seq, 0)),
        ],
        out_specs=pl.BlockSpec((tm, tn), lambda i, j: (i, j)),
        out_shape=jax.ShapeDtypeStruct((nt, n_cols * tn), BF16),
        scratch_shapes=[pltpu.VMEM((tm, d), BF16)],
        compiler_params=_params("parallel", "arbitrary"),
        name="even_proj_qk" if rope else "even_proj_rest",
    )(h, g, w_in, cos128, sin128)


def _store_scores(q, k_ref, tail_bias, s_ref):
    s = _nt_dot(q, k_ref[...])
    n_main = s.shape[1] - LANES
    s_ref[:, :n_main] = s[:, :n_main]
    s_ref[:, n_main:] = s[:, n_main:] + tail_bias


def _probs(s_ref):
    s = s_ref[...]
    p = jnp.exp2(s - jnp.max(s, axis=-1, keepdims=True))
    return p, jnp.sum(p, axis=-1, keepdims=True)


def _tail_bias(seq, lp):
    lane = lax.broadcasted_iota(jnp.int32, (1, LANES), 1)
    return jnp.where(lane < seq - (lp - LANES), 0.0, NEG_BIG).astype(F32)


def _nt_dot(a, b):
    return lax.dot_general(a, b, (((1,), (1,)), ((), ())), preferred_element_type=F32)


def _attention_pipeline(n_tiles, scores, softmax, pv):
    sums = {}
    for step in range(n_tiles + 2):
        if 0 <= step - 2 < n_tiles:
            pv(step - 2, (step - 2) % 2, sums.pop(step - 2))
        if 0 <= step - 1 < n_tiles:
            sums[step - 1] = softmax((step - 1) % 2)
        if step < n_tiles:
            scores(step, step % 2)


def _diff_attn_kernel(lq1_ref, lk1_ref, lq2_ref, lk2_ref, q_ref, k_ref, v_ref, g_ref, o_ref, kpad, vpad,
                      s1a, s1b, s2a, s2b, pa, pb, *, seq, lam_init, tq):
    lp = kpad.shape[0]
    kpad[0:seq, :] = k_ref[0]
    kpad[seq:lp, :] = jnp.zeros((lp - seq, LANES), BF16)
    vpad[0:seq, :] = v_ref[0]
    vpad[seq:lp, :] = jnp.zeros((lp - seq, LANES), BF16)
    lam = (jnp.exp(jnp.sum(lq1_ref[...] * lk1_ref[...], axis=-1, keepdims=True))
           - jnp.exp(jnp.sum(lq2_ref[...] * lk2_ref[...], axis=-1, keepdims=True)) + lam_init)
    tail_bias = _tail_bias(seq, lp)
    gain = g_ref[...] * (1.0 - lam_init)
    s_bufs, p_bufs = ((s1a, s2a), (s1b, s2b)), (pa, pb)

    def scores(t, par):
        q = q_ref[0, t * tq:(t + 1) * tq, :]
        lane = lax.broadcasted_iota(jnp.int32, q.shape, 1)
        zero = jnp.zeros_like(q)
        _store_scores(jnp.where(lane < HEAD_DIM, q, zero), kpad, tail_bias, s_bufs[par][0])
        _store_scores(jnp.where(lane >= HEAD_DIM, q, zero), kpad, tail_bias, s_bufs[par][1])

    def softmax(par):
        p1, l1 = _probs(s_bufs[par][0])
        p2, l2 = _probs(s_bufs[par][1])
        p_bufs[par][...] = (p1 * (1.0 / l1) - p2 * (lam / l2)).astype(BF16)

    def pv(t, par, _):
        o = jnp.dot(p_bufs[par][...], vpad[...], preferred_element_type=F32)
        o = o * lax.rsqrt(jnp.mean(o * o, axis=-1, keepdims=True) + EPS) * gain
        o_ref[0, t * tq:(t + 1) * tq, :] = o.astype(o_ref.dtype)

    _attention_pipeline(seq // tq, scores, softmax, pv)


def _diff_attn(qk3, rest3, lq1, lk1, lq2, lk2, subln_g, lam_init, heads):
    nb, seq, _ = qk3.shape
    lp = pl.cdiv(seq, LANES) * LANES
    assert lp > seq
    tq = _tile(seq, ATTN_Q_TILE_MAX, 16)
    vec = lambda: pl.BlockSpec((1, HEAD_DIM), lambda b, h: (0, 0))
    return pl.pallas_call(
        functools.partial(_diff_attn_kernel, seq=seq, lam_init=lam_init, tq=tq),
        grid=(nb, heads),
        in_specs=[
            vec(), vec(), vec(), vec(),
            pl.BlockSpec((1, seq, LANES), lambda b, h: (b, 0, h)),
            pl.BlockSpec((1, seq, LANES), lambda b, h: (b, 0, heads + h)),
            pl.BlockSpec((1, seq, LANES), lambda b, h: (b, 0, h)),
            pl.BlockSpec((1, LANES), lambda b, h: (0, 0)),
        ],
        out_specs=pl.BlockSpec((1, seq, LANES), lambda b, h: (b, 0, h)),
        out_shape=jax.ShapeDtypeStruct((nb, seq, heads * LANES), BF16),
        scratch_shapes=[pltpu.VMEM((lp, LANES), BF16), pltpu.VMEM((lp, LANES), BF16)]
        + [pltpu.VMEM((tq, lp), F32)] * 4 + [pltpu.VMEM((tq, lp), BF16)] * 2,
        compiler_params=_params("parallel", "parallel"),
        name="diff_attn",
    )(lq1, lk1, lq2, lk2, qk3, qk3, rest3, subln_g)


def _conv_kernel(gb_ref, gc_ref, cx_ref, w_ref, o_ref):
    u = gc_ref[0].astype(F32) * cx_ref[0].astype(F32)
    seq = u.shape[0]
    row = lax.broadcasted_iota(jnp.int32, u.shape, 0)
    prev = jnp.where(row == 0, 0.0, pltpu.roll(u, 1, 0))
    nxt = jnp.where(row == seq - 1, 0.0, pltpu.roll(u, seq - 1, 0))
    y = prev * w_ref[0:1, :] + u * w_ref[1:2, :] + nxt * w_ref[2:3, :]
    o_ref[0] = (gb_ref[0].astype(F32) * y).astype(o_ref.dtype)


def _gated_conv(rest3, conv_w, width):
    nb, seq, _ = rest3.shape
    tc = _tile(width, 256, LANES)
    nc = width // tc
    blk = lambda part: pl.BlockSpec((1, seq, tc), lambda b, c: (b, 0, part * nc + c))
    return pl.pallas_call(
        _conv_kernel,
        grid=(nb, nc),
        in_specs=[blk(1), blk(2), blk(3), pl.BlockSpec((3, tc), lambda b, c: (0, c))],
        out_specs=pl.BlockSpec((1, seq, tc), lambda b, c: (b, 0, c)),
        out_shape=jax.ShapeDtypeStruct((nb, seq, width), BF16),
        compiler_params=_params("parallel", "parallel"),
        name="gated_conv",
    )(rest3, rest3, rest3, conv_w)


def _out_proj_router_kernel(a1_ref, a2_ref, w_ref, h_ref, g_ref, wr_ref, hn_ref, tn_ref, aff_ref):
    k1 = a1_ref.shape[1]
    y = jnp.dot(a1_ref[...], w_ref[0:k1, :], preferred_element_type=F32)
    y = y + jnp.dot(a2_ref[...], w_ref[k1:, :], preferred_element_type=F32)
    h = h_ref[...] + y
    hn_ref[...] = h
    t = _rms(h, g_ref[...])
    d = t.shape[1]
    n_exp = aff_ref.shape[1]
    logits = jnp.dot(t.astype(BF16), wr_ref[...], preferred_element_type=F32)
    lane = lax.broadcasted_iota(jnp.int32, logits.shape, 1)
    logits = jnp.where(lane < n_exp, logits, NEG_BIG)
    e = jnp.exp(logits - jnp.max(logits, axis=-1, keepdims=True))
    aff = e / jnp.sum(e, axis=-1, keepdims=True)
    aff_ref[...] = aff[:, :n_exp]
    tn_ref[:, :d] = t
    tn_ref[:, d:] = aff


def _out_proj_router(a1, a2, col1, col2, w_out, h, ffn_g, w_router):
    nt, d = h.shape
    k1 = w_out.shape[0] // 2
    n_exp = w_router.shape[1]
    w_router = jnp.pad(w_router, ((0, 0), (0, LANES - n_exp)))
    tm = _tile(nt, 512, 16)
    row = lambda c: (lambda i: (i, c))
    const = lambda i: (0, 0)
    return pl.pallas_call(
        _out_proj_router_kernel,
        grid=(nt // tm,),
        in_specs=[
            pl.BlockSpec((tm, k1), row(col1)),
            pl.BlockSpec((tm, k1), row(col2)),
            pl.BlockSpec(w_out.shape, const),
            pl.BlockSpec((tm, d), row(0)),
            pl.BlockSpec((1, d), const),
            pl.BlockSpec(w_router.shape, const),
        ],
        out_specs=[
            pl.BlockSpec((tm, d), row(0)),
            pl.BlockSpec((tm, d + LANES), row(0)),
            pl.BlockSpec((tm, n_exp), row(0)),
        ],
        out_shape=[
            jax.ShapeDtypeStruct((nt, d), F32),
            jax.ShapeDtypeStruct((nt, d + LANES), F32),
            jax.ShapeDtypeStruct((nt, n_exp), F32),
        ],
        compiler_params=_params("parallel"),
        name="out_proj_router",
    )(a1, a2, w_out, h, ffn_g, w_router)


def _odd_in_kernel(x_ref, g_ref, w_ref, qg_ref, kvg_ref, cos_ref, sin_ref, cq_ref, ckv_ref, kr_ref):
    xn = _rms(x_ref[...], g_ref[...]).astype(BF16)
    p = jnp.dot(xn, w_ref[...], preferred_element_type=F32)
    qr, kvr = cq_ref.shape[1], ckv_ref.shape[1]
    cq_ref[...] = _rms(p[:, :qr], qg_ref[...]).astype(cq_ref.dtype)
    ckv_ref[...] = _rms(p[:, qr: qr + kvr], kvg_ref[...]).astype(ckv_ref.dtype)
    kr_ref[...] = _rope128(p[:, qr + kvr:], cos_ref[...], sin_ref[...]).astype(kr_ref.dtype)


def _odd_in(h, g, w_in_pad, q_g, kv_g, cos128, sin128, seq):
    nt, d = h.shape
    qr, kvr = q_g.shape[1], kv_g.shape[1]
    tm = _tile(seq, 768, 16)
    tiles_per_seq = seq // tm
    row = lambda i: (i, 0)
    const = lambda i: (0, 0)
    pos = lambda i: (i % tiles_per_seq, 0)
    return pl.pallas_call(
        _odd_in_kernel,
        grid=(nt // tm,),
        in_specs=[
            pl.BlockSpec((tm, d), row),
            pl.BlockSpec((1, d), const),
            pl.BlockSpec(w_in_pad.shape, const),
            pl.BlockSpec((1, qr), const),
            pl.BlockSpec((1, kvr), const),
            pl.BlockSpec((tm, LANES), pos),
            pl.BlockSpec((tm, LANES), pos),
        ],
        out_specs=[
            pl.BlockSpec((tm, qr), row),
            pl.BlockSpec((tm, kvr), row),
            pl.BlockSpec((tm, LANES), row),
        ],
        out_shape=[
            jax.ShapeDtypeStruct((nt, qr), BF16),
            jax.ShapeDtypeStruct((nt, kvr), BF16),
            jax.ShapeDtypeStruct((nt, LANES), BF16),
        ],
        compiler_params=_params("parallel"),
        name="odd_in",
    )(h, g, w_in_pad, q_g, kv_g, cos128, sin128)


def _up_proj_kernel(a_ref, w_ref, cos_ref, sin_ref, o_ref, *, rope, scale):
    a = a_ref[...]
    cos, sin = cos_ref[...], sin_ref[...]
    for c in range(o_ref.shape[1] // MLA_QK_PAD):
        lo = c * MLA_QK_PAD
        acc = jnp.dot(a, w_ref[:, lo: lo + MLA_QK_PAD], preferred_element_type=F32)
        if rope:
            o_ref[:, lo: lo + LANES] = (acc[:, :LANES] * scale).astype(o_ref.dtype)
            o_ref[:, lo + LANES: lo + MLA_QK_PAD] = (_rope128(acc[:, LANES:], cos, sin) * scale).astype(o_ref.dtype)
        else:
            o_ref[:, lo: lo + MLA_QK_PAD] = acc.astype(o_ref.dtype)


def _up_proj(a, w, cos128, sin128, seq, rope, scale):
    nt, k = a.shape
    n = w.shape[1]
    tm = _tile(seq, 768, 16)
    tiles_per_seq = seq // tm
    pos = lambda i: (i % tiles_per_seq, 0)
    return pl.pallas_call(
        functools.partial(_up_proj_kernel, rope=rope, scale=scale),
        grid=(nt // tm,),
        in_specs=[
            pl.BlockSpec((tm, k), lambda i: (i, 0)),
            pl.BlockSpec((k, n), lambda i: (0, 0)),
            pl.BlockSpec((tm, LANES), pos),
            pl.BlockSpec((tm, LANES), pos),
        ],
        out_specs=pl.BlockSpec((tm, n), lambda i: (i, 0)),
        out_shape=jax.ShapeDtypeStruct((nt, n), BF16),
        compiler_params=_params("parallel"),
        name="q_up_proj" if rope else "kv_up_proj",
    )(a, w, cos128, sin128)


def _mla_attn_kernel(q_ref, kn_ref, kr_ref, v_ref, o_ref, kcat, vpad, sa, sb, pa, pb, *, seq, tq):
    lp = kcat.shape[0]
    kcat[0:seq, 0:LANES] = kn_ref[0]
    kcat[0:seq, LANES:] = kr_ref[0]
    kcat[seq:lp, :] = jnp.zeros((lp - seq, MLA_QK_PAD), BF16)
    vpad[0:seq, :] = v_ref[0]
    vpad[seq:lp, :] = jnp.zeros((lp - seq, LANES), BF16)
    tail_bias = _tail_bias(seq, lp)

    s_bufs, p_bufs = (sa, sb), (pa, pb)

    def scores(t, par):
        _store_scores(q_ref[0, t * tq:(t + 1) * tq, :], kcat, tail_bias, s_bufs[par])

    def softmax(par):
        p, l = _probs(s_bufs[par])
        p_bufs[par][...] = p.astype(BF16)
        return l

    def pv(t, par, l):
        o = jnp.dot(p_bufs[par][...], vpad[...], preferred_element_type=F32) / l
        o_ref[0, t * tq:(t + 1) * tq, :] = o.astype(o_ref.dtype)

    _attention_pipeline(seq // tq, scores, softmax, pv)


def _mla_attn(q3, kv3, kr3, heads):
    nb, seq, _ = q3.shape
    lp = pl.cdiv(seq, LANES) * LANES
    assert lp > seq
    tq = _tile(seq, ATTN_Q_TILE_MAX, 16)
    return pl.pallas_call(
        functools.partial(_mla_attn_kernel, seq=seq, tq=tq),
        grid=(nb, heads),
        in_specs=[
            pl.BlockSpec((1, seq, MLA_QK_PAD), lambda b, h: (b, 0, h)),
            pl.BlockSpec((1, seq, LANES), lambda b, h: (b, 0, h)),
            pl.BlockSpec((1, seq, LANES), lambda b, h: (b, 0, 0)),
            pl.BlockSpec((1, seq, LANES), lambda b, h: (b, 0, heads + h)),
        ],
        out_specs=pl.BlockSpec((1, seq, LANES), lambda b, h: (b, 0, h)),
        out_shape=jax.ShapeDtypeStruct((nb, seq, heads * LANES), BF16),
        scratch_shapes=[pltpu.VMEM((lp, MLA_QK_PAD), BF16), pltpu.VMEM((lp, LANES), BF16)]
        + [pltpu.VMEM((tq, lp), F32)] * 2 + [pltpu.VMEM((tq, lp), BF16)] * 2,
        compiler_params=_params("parallel", "parallel"),
        name="mla_attn",
    )(q3, kv3, kr3, kv3)


def _moe_ffn_kernel(idx_ref, tn_hbm, wg_ref, wu_ref, wd_ref, o_ref, xs_buf, sem, *, tm):
    step = pl.program_id(0) * pl.num_programs(1) + pl.program_id(1)
    n_steps = pl.num_programs(0) * pl.num_programs(1)
    slot = step % 2

    def gather(tile, into):
        for r in range(tm):
            tok = idx_ref[tile * tm + r]
            pltpu.make_async_copy(tn_hbm.at[pl.ds(tok, 1), :], xs_buf.at[into, pl.ds(r, 1), :], sem.at[into]).start()

    def wait(which):
        pltpu.make_async_copy(tn_hbm.at[pl.ds(0, tm), :], xs_buf.at[which], sem.at[which]).wait()

    @pl.when(step == 0)
    def _():
        gather(0, 0)

    wait(slot)
    d = o_ref.shape[1]
    xs = xs_buf[slot, :, :d].astype(BF16)
    aff = xs_buf[slot, :, d:]
    lane = lax.broadcasted_iota(jnp.int32, aff.shape, 1)
    gate = jnp.sum(jnp.where(lane == pl.program_id(0), aff, 0.0), axis=-1, keepdims=True)
    hg = jnp.dot(xs, wg_ref[...], preferred_element_type=F32)
    hu = jnp.dot(xs, wu_ref[...], preferred_element_type=F32)
    hid = (hg * jax.nn.sigmoid(hg) * hu).astype(BF16)
    gather(step + 1, 1 - slot)
    o_ref[...] = (jnp.dot(hid, wd_ref[...], preferred_element_type=F32) * gate).astype(o_ref.dtype)

    @pl.when(step == n_steps - 1)
    def _():
        wait(1 - slot)


def _moe_ffn(idx_flat, tn, w_gate, w_up, w_down, slots):
    n_exp, d, ff = w_gate.shape
    tm = _tile(slots, FFN_ROWS_MAX, 16)
    tiles = slots // tm
    grid_spec = pltpu.PrefetchScalarGridSpec(
        num_scalar_prefetch=1,
        grid=(n_exp, tiles),
        in_specs=[
            pl.BlockSpec(memory_space=pl.ANY),
            pl.BlockSpec((None, d, ff), lambda e, m, idx: (e, 0, 0)),
            pl.BlockSpec((None, d, ff), lambda e, m, idx: (e, 0, 0)),
            pl.BlockSpec((None, ff, d), lambda e, m, idx: (e, 0, 0)),
        ],
        out_specs=pl.BlockSpec((tm, d), lambda e, m, idx: (e * tiles + m, 0)),
        scratch_shapes=[pltpu.VMEM((2, tm, tn.shape[1]), F32), pltpu.SemaphoreType.DMA((2,))],
    )
    idx_padded = jnp.concatenate([idx_flat, idx_flat[:tm]])
    return pl.pallas_call(
        functools.partial(_moe_ffn_kernel, tm=tm),
        grid_spec=grid_spec,
        out_shape=jax.ShapeDtypeStruct((n_exp * slots, d), BF16),
        compiler_params=_params("arbitrary", "arbitrary"),
        name="moe_ffn",
    )(idx_padded, tn, w_gate, w_up, w_down)


def _combine_kernel(src_ref, pieces_ref, dst_ref, chunks_ref, ye_hbm, h_ref, tgt_ref, o_ref, stage, tgt_b, sem):
    i = pl.program_id(0)
    n_tiles = pl.num_programs(0)
    n_exp = tgt_ref.shape[1]
    slot = i % 2

    @pl.when(i == 0)
    def _():
        stage[...] = jnp.zeros(stage.shape, stage.dtype)

    def piece(src_row, into, dst_row):
        return pltpu.make_async_copy(ye_hbm.at[pl.ds(src_row, COMBINE_PIECE), :],
                                     stage.at[into, pl.ds(dst_row, COMBINE_PIECE), :], sem.at[into])

    def fetch(tile, into, live):
        n_pieces = 0
        for e in range(n_exp):
            src, dst = src_ref[tile * n_exp + e], dst_ref[tile * n_exp + e]
            n = pieces_ref[tile * n_exp + e] * live

            def issue(p, carry, src=src, dst=dst):
                piece(pl.multiple_of(src + p * COMBINE_PIECE, COMBINE_PIECE), into,
                      pl.multiple_of(dst + p * COMBINE_PIECE, COMBINE_PIECE)).start()
                return carry

            lax.fori_loop(0, n, issue, 0)
            n_pieces = n_pieces + n
        return n_pieces

    @pl.when(i == 0)
    def _():
        fetch(0, 0, 1)

    n_mine = 0
    for e in range(n_exp):
        n_mine = n_mine + pieces_ref[i * n_exp + e]

    def wait(p, carry):
        piece(0, slot, 0).wait()
        return carry

    lax.fori_loop(0, n_mine, wait, 0)
    nxt = jnp.minimum(i + 1, n_tiles - 1)
    fetch(nxt, 1 - slot, jnp.where(i + 1 < n_tiles, 1, 0))
    o_ref[...] = h_ref[...]
    t_tile = tgt_ref.shape[0]
    for e in range(n_exp):
        tgt_b[e] = jnp.broadcast_to(tgt_ref[:, e:e + 1], (t_tile, LANES))
    lane = lax.broadcasted_iota(jnp.int32, (1, LANES), 1)

    def chunk(c, carry):
        base = c * COMBINE_CHUNK
        parts = []
        for part in range(COMBINE_CHUNK // LANES):
            want = lane + (base + part * LANES)
            onehot = jnp.zeros((t_tile, LANES), F32)
            for e in range(n_exp):
                onehot = jnp.where(tgt_b[e] == want, 1.0, onehot)
            parts.append(onehot.astype(BF16))
        rows = stage[slot, pl.ds(pl.multiple_of(base, COMBINE_CHUNK), COMBINE_CHUNK), :]
        o_ref[...] += jnp.dot(jnp.concatenate(parts, axis=1), rows, preferred_element_type=F32)
        return carry

    lax.fori_loop(0, chunks_ref[i], chunk, 0)


def _combine(ye, h, sel_t, row_t, t_tile):
    nt, d = h.shape
    n_exp = sel_t.shape[1]
    assert ye.shape[0] % COMBINE_PIECE == 0
    first = row_t[::t_tile]
    last = (row_t + sel_t)[t_tile - 1::t_tile]
    src = first // COMBINE_PIECE * COMBINE_PIECE
    pieces = (last - src + COMBINE_PIECE - 1) // COMBINE_PIECE
    dst = (jnp.cumsum(pieces, axis=1) - pieces) * COMBINE_PIECE
    chunks = (jnp.sum(pieces, axis=1) * COMBINE_PIECE + COMBINE_CHUNK - 1) // COMBINE_CHUNK
    tgt = jnp.where(sel_t > 0, row_t - jnp.repeat(src - dst, t_tile, axis=0), -1)
    stage_rows = pl.cdiv(n_exp * (t_tile + 2 * (COMBINE_PIECE - 1)), COMBINE_CHUNK) * COMBINE_CHUNK
    grid_spec = pltpu.PrefetchScalarGridSpec(
        num_scalar_prefetch=4,
        grid=(nt // t_tile,),
        in_specs=[
            pl.BlockSpec(memory_space=pl.ANY),
            pl.BlockSpec((t_tile, d), lambda i, *_: (i, 0)),
            pl.BlockSpec((t_tile, n_exp), lambda i, *_: (i, 0)),
        ],
        out_specs=pl.BlockSpec((t_tile, d), lambda i, *_: (i, 0)),
        scratch_shapes=[pltpu.VMEM((2, stage_rows, d), ye.dtype), pltpu.VMEM((n_exp, t_tile, LANES), jnp.int32),
                        pltpu.SemaphoreType.DMA((2,))],
    )
    flat = lambda x: x.reshape(-1).astype(jnp.int32)
    return pl.pallas_call(
        _combine_kernel,
        grid_spec=grid_spec,
        out_shape=jax.ShapeDtypeStruct((nt, d), F32),
        compiler_params=_params("arbitrary"),
        name="moe_combine",
    )(flat(src), flat(pieces), flat(dst), flat(chunks), ye, h, tgt.astype(jnp.int32))


def _route_kernel(x_ref, sel_ref, pexc_ref, idx_ref, *, cap, slot_block):
    x = x_ref[0]
    n_rows = x.shape[0]
    bits = pltpu.bitcast(x, jnp.int32)
    cap_f = float(cap)

    def count(mask):
        return jnp.sum(jnp.where(mask, 1.0, 0.0), axis=(0, 1), keepdims=True)

    def bisect(_, bounds):
        lo, hi = bounds
        mid = lo + ((hi - lo) >> 1)
        ok = count(bits >= mid) >= cap_f
        return jnp.where(ok, mid, lo), jnp.where(ok, hi, mid)

    tau, _ = lax.fori_loop(0, 31, bisect, (jnp.zeros((1, 1), jnp.int32), jnp.full((1, 1), F32_INF_BITS, jnp.int32)))

    upper = jnp.where(lax.broadcasted_iota(jnp.int32, (LANES, LANES), 0)
                      <= lax.broadcasted_iota(jnp.int32, (LANES, LANES), 1), 1.0, 0.0).astype(BF16)
    lower = jnp.where(lax.broadcasted_iota(jnp.int32, (n_rows, n_rows), 1)
                      < lax.broadcasted_iota(jnp.int32, (n_rows, n_rows), 0), 1.0, 0.0).astype(BF16)

    def prefix(m):
        within = jnp.dot(m.astype(BF16), upper, preferred_element_type=F32)
        row_tot = jnp.broadcast_to(within[:, LANES - 1:], within.shape)
        before = jnp.dot(lower, row_tot.astype(BF16), preferred_element_type=F32)
        return within, before, row_tot

    gt = jnp.where(bits > tau, 1.0, 0.0)
    eq = jnp.where(bits == tau, 1.0, 0.0)
    need = cap_f - jnp.sum(gt, axis=(0, 1), keepdims=True)
    tie_within, tie_before, _ = prefix(eq)
    tie_rank = tie_within + tie_before - eq
    sel = gt + eq * jnp.where(tie_rank < need, 1.0, 0.0)
    within, before, row_tot = prefix(sel)
    sel_ref[0] = sel.astype(jnp.int32)
    pexc_ref[0] = (within + before - sel).astype(jnp.int32)

    row_end = jnp.transpose(before + row_tot)[0:1, :]
    row_start = jnp.transpose(before)[0:1, :]
    within_b = within.astype(BF16)
    row_id = lax.broadcasted_iota(jnp.int32, (1, n_rows), 1).astype(F32)
    for s in range(idx_ref.shape[1] // slot_block):
        j = (lax.broadcasted_iota(jnp.int32, (slot_block, 1), 0) + s * slot_block).astype(F32)
        row = jnp.sum(jnp.where(row_end <= j, 1.0, 0.0), axis=-1, keepdims=True)
        onehot = jnp.where(row_id == row, 1.0, 0.0)
        cum = jnp.dot(onehot.astype(BF16), within_b, preferred_element_type=F32)
        local = j - jnp.sum(onehot * row_start, axis=-1, keepdims=True)
        col = jnp.sum(jnp.where(cum <= local, 1.0, 0.0), axis=-1, keepdims=True)
        tok = (row * LANES + col).astype(jnp.int32)
        idx_ref[0, s * slot_block:(s + 1) * slot_block, :] = jnp.where(j < cap_f, tok, 0)


def _route(aff_rows, cap):
    n_exp, n_rows, _ = aff_rows.shape
    cap_pad = pl.cdiv(cap, ROUTE_SLOT_BLOCK) * ROUTE_SLOT_BLOCK
    blk = pl.BlockSpec((1, n_rows, LANES), lambda e: (e, 0, 0))
    return pl.pallas_call(
        functools.partial(_route_kernel, cap=cap, slot_block=ROUTE_SLOT_BLOCK),
        grid=(n_exp,),
        in_specs=[blk],
        out_specs=[blk, blk, pl.BlockSpec((1, cap_pad, 1), lambda e: (e, 0, 0))],
        out_shape=[
            jax.ShapeDtypeStruct(aff_rows.shape, jnp.int32),
            jax.ShapeDtypeStruct(aff_rows.shape, jnp.int32),
            jax.ShapeDtypeStruct((n_exp, cap_pad, 1), jnp.int32),
        ],
        compiler_params=_params("parallel"),
        name="route",
    )(aff_rows)


def _moe(h, tn, aff, group_sizes, w_gate, w_up, w_down):
    n_exp = aff.shape[1]
    caps = [N_EXPERT_CAPACITY_FACTOR * n // n_exp for n in group_sizes]
    slots = pl.cdiv(sum(caps), FFN_SLOT_ALIGN) * FFN_SLOT_ALIGN
    idx_parts, sel_parts, row_parts = [], [], []
    tok0, slot0 = 0, 0
    for n, cap in zip(group_sizes, caps):
        n_rows = pl.cdiv(pl.cdiv(n, LANES), LANES) * LANES
        rows = jnp.pad(aff[tok0: tok0 + n].T, ((0, 0), (0, n_rows * LANES - n)), constant_values=-1.0)
        sel, pexc, idx = _route(rows.reshape(n_exp, n_rows, LANES), cap)
        idx_parts.append(idx[:, :cap, 0] + tok0)
        sel_parts.append(sel.reshape(n_exp, -1)[:, :n].T)
        expert_row0 = jnp.arange(n_exp, dtype=jnp.int32)[None, :] * slots + slot0
        row_parts.append(pexc.reshape(n_exp, -1)[:, :n].T + expert_row0)
        tok0 += n
        slot0 += cap
    idx_parts.append(jnp.zeros((n_exp, slots - sum(caps)), jnp.int32))
    idx_all = jnp.concatenate(idx_parts, axis=1)
    sel_t = jnp.concatenate(sel_parts, axis=0)
    row_t = jnp.concatenate(row_parts, axis=0)
    ye = _moe_ffn(idx_all.reshape(-1), tn, w_gate, w_up, w_down, slots)
    t_tile = _tile(math.gcd(*group_sizes), COMBINE_TOKENS_MAX, 8)
    return _combine(ye, h, sel_t, row_t, t_tile)


def _final_norm_kernel(x_ref, g_ref, o_ref):
    o_ref[...] = _rms(x_ref[...], g_ref[...])


def _final_norm(h, g, first_seq, n_seqs, seq):
    _, d = h.shape
    s_out = seq - N_META
    tm = _tile(s_out, 512, 8)
    per_seq = s_out // tm
    out = pl.pallas_call(
        _final_norm_kernel,
        grid=(n_seqs, per_seq),
        in_specs=[
            pl.BlockSpec((pl.Element(tm), pl.Element(d)),
                         lambda b, j: (pl.multiple_of((first_seq + b) * seq + N_META + j * tm, 8), 0)),
            pl.BlockSpec((1, d), lambda b, j: (0, 0)),
        ],
        out_specs=pl.BlockSpec((tm, d), lambda b, j: (b * per_seq + j, 0)),
        out_shape=jax.ShapeDtypeStruct((n_seqs * s_out, d), F32),
        compiler_params=_params("parallel", "parallel"),
        name="final_norm",
    )(h, g)
    return out.reshape(n_seqs, s_out, d)


def _lambda_init(layer):
    return 0.8 - 0.6 * math.exp(-0.3 * layer)


def kernel(x_prompt, x_sample, meta_tokens, mix_norm_g, ffn_norm_g, final_norm_g, ev_w_in, ev_w_out, ev_lambda_q1, ev_lambda_k1, ev_lambda_q2, ev_lambda_k2, ev_subln_g, ev_conv_w, od_w_in, od_q_norm_g, od_w_q_b, od_kv_norm_g, od_w_kv_b, od_w_out, moe_w_router, moe_w_gate, moe_w_up, moe_w_down):
    d = x_prompt.shape[-1]
    seq = N_META + x_prompt.shape[1]
    assert x_sample.shape[1] == x_prompt.shape[1]
    nb_p, nb_s = x_prompt.shape[0], x_sample.shape[0]
    nb = nb_p + nb_s
    nt = nb * seq
    group_sizes = (nb_p * seq, nb_s * seq)
    depth = mix_norm_g.shape[0]

    x = jnp.concatenate([x_prompt, x_sample], axis=0)
    meta = jnp.broadcast_to(meta_tokens[None].astype(x.dtype), (nb, N_META, d))
    h = jnp.concatenate([meta, x], axis=1).reshape(nt, d)
    cos128, sin128 = _rope_tables(seq)

    for layer in range(depth):
        i = layer // 2
        mix_g = mix_norm_g[layer][None]
        ffn_g = ffn_norm_g[layer][None]
        w_router = moe_w_router[layer].astype(BF16)
        if layer % 2 == 0:
            width = d // 2
            heads = width // LANES
            w_in = ev_w_in[i].astype(BF16)
            qk3 = _even_proj(h, mix_g, w_in, cos128, sin128, seq, True).reshape(nb, seq, 2 * width)
            rest3 = _even_proj(h, mix_g, w_in, cos128, sin128, seq, False).reshape(nb, seq, 4 * width)
            attn = _diff_attn(qk3, rest3, ev_lambda_q1[i][None], ev_lambda_k1[i][None], ev_lambda_q2[i][None],
                              ev_lambda_k2[i][None], ev_subln_g[i][None], _lambda_init(layer), heads)
            conv = _gated_conv(rest3, ev_conv_w[i], width)
            h, tn, aff = _out_proj_router(attn.reshape(nt, width), conv.reshape(nt, width), 0, 0,
                                          ev_w_out[i].astype(BF16), h, ffn_g, w_router)
        else:
            q_rank, kv_rank = od_q_norm_g.shape[1], od_kv_norm_g.shape[1]
            heads = od_w_q_b.shape[2] // (MLA_NOPE + MLA_ROPE)
            w_in_pad = jnp.pad(od_w_in[i], ((0, 0), (0, LANES - MLA_ROPE))).astype(BF16)
            w_q = od_w_q_b[i].reshape(q_rank, heads, MLA_NOPE + MLA_ROPE)
            w_q = jnp.pad(w_q, ((0, 0), (0, 0), (0, MLA_QK_PAD - MLA_NOPE - MLA_ROPE)))
            w_q = w_q.reshape(q_rank, heads * MLA_QK_PAD).astype(BF16)
            w_kv = od_w_kv_b[i].reshape(kv_rank, heads, MLA_NOPE + MLA_V)
            w_kv = jnp.concatenate([w_kv[:, :, :MLA_NOPE].reshape(kv_rank, heads * MLA_NOPE),
                                    w_kv[:, :, MLA_NOPE:].reshape(kv_rank, heads * MLA_V)], axis=1).astype(BF16)
            cq, ckv, kr = _odd_in(h, mix_g, w_in_pad, od_q_norm_g[i][None], od_kv_norm_g[i][None], cos128, sin128, seq)
            q = _up_proj(cq, w_q, cos128, sin128, seq, True, (MLA_NOPE + MLA_ROPE) ** -0.5 * LOG2E)
            kv = _up_proj(ckv, w_kv, cos128, sin128, seq, False, 1.0)
            o = _mla_attn(q.reshape(nb, seq, heads * MLA_QK_PAD), kv.reshape(nb, seq, 2 * heads * LANES),
                          kr.reshape(nb, seq, LANES), heads)
            o = o.reshape(nt, heads * MLA_V)
            h, tn, aff = _out_proj_router(o, o, 0, 1, od_w_out[i].astype(BF16), h, ffn_g, w_router)
        h = _moe(h, tn, aff, group_sizes, moe_w_gate[layer].astype(BF16), moe_w_up[layer].astype(BF16),
                 moe_w_down[layer].astype(BF16))

    g = final_norm_g[None]
    return (_final_norm(h, g, 0, nb_p, seq), _final_norm(h, g, nb_p, nb_s, seq))
```

```python
import functools
import math

import jax
import jax.numpy as jnp
from jax import lax
from jax.experimental import pallas as pl
from jax.experimental.pallas import tpu as pltpu

N_META = 16
EPS = 1e-6
ROPE_THETA = 10000.0
HEAD_DIM = 64
LANES = 128
MXU_COLS = 256
MLA_NOPE = 128
MLA_ROPE = 64
MLA_V = 128
MLA_QK_PAD = 256
ATTN_Q_TILE_MAX = 768
N_EXPERT_CAPACITY_FACTOR = 2
VMEM_LIMIT_BYTES = 56 * 1024 * 1024
NEG_BIG = -1e30
F32_INF_BITS = 0x7F800000
LOG2E = math.log2(math.e)
ROUTE_SLOT_BLOCK = 512
COMBINE_TOKENS_MAX = 192
COMBINE_PIECE = 16
FFN_ROWS_MAX = 448
FFN_SLOT_ALIGN = 64
COMBINE_CHUNK = 256

F32 = jnp.float32
BF16 = jnp.bfloat16


def _tile(n, cap, mult):
    best = None
    for t in range(mult, min(n, cap) + 1, mult):
        if n % t == 0:
            best = t
    assert best is not None, (n, cap, mult)
    return best


def _params(*sem):
    return pltpu.CompilerParams(dimension_semantics=sem, vmem_limit_bytes=VMEM_LIMIT_BYTES)


def _rms(x, g):
    return x * lax.rsqrt(jnp.mean(x * x, axis=-1, keepdims=True) + EPS) * g


def _rope128(x, cos, sin_signed):
    lane = lax.broadcasted_iota(jnp.int32, x.shape, 1)
    first_half = (lane % HEAD_DIM) < (HEAD_DIM // 2)
    rot = jnp.where(first_half, pltpu.roll(x, LANES - HEAD_DIM // 2, 1), pltpu.roll(x, HEAD_DIM // 2, 1))
    return x * cos + rot * sin_signed


def _rope_tables(length):
    inv = 1.0 / (ROPE_THETA ** (jnp.arange(0, HEAD_DIM, 2, dtype=F32) / HEAD_DIM))
    ang = jnp.arange(length, dtype=F32)[:, None] * inv[None, :]
    cos, sin = jnp.cos(ang), jnp.sin(ang)
    cos128 = jnp.tile(cos, (1, LANES // (HEAD_DIM // 2)))
    sin128 = jnp.tile(jnp.concatenate([-sin, sin], axis=1), (1, LANES // HEAD_DIM))
    return cos128, sin128


def _even_proj_kernel(x_ref, g_ref, w_ref, cos_ref, sin_ref, o_ref, xn_ref, *, rope, q_scale):
    j = pl.program_id(1)

    @pl.when(j == 0)
    def _():
        xn_ref[...] = _rms(x_ref[...], g_ref[...]).astype(BF16)

    xn = xn_ref[...]
    if rope:
        scale = jnp.where(j == 0, q_scale, 1.0).astype(F32)
        cos, sin = cos_ref[...], sin_ref[...]
    for c in range(o_ref.shape[1] // MXU_COLS):
        lo = c * MXU_COLS
        acc = jnp.dot(xn, w_ref[:, lo: lo + MXU_COLS], preferred_element_type=F32)
        if rope:
            for half in range(MXU_COLS // LANES):
                sl = slice(half * LANES, (half + 1) * LANES)
                o_ref[:, lo + half * LANES: lo + (half + 1) * LANES] = (
                    _rope128(acc[:, sl], cos, sin) * scale).astype(o_ref.dtype)
        else:
            o_ref[:, lo: lo + MXU_COLS] = acc.astype(o_ref.dtype)


def _even_proj(h, g, w_in, cos128, sin128, seq, rope):
    nt, d = h.shape
    tn = w_in.shape[1] // 6
    col0, n_cols = (0, 2) if rope else (2, 4)
    tm = _tile(seq, 768, 16)
    tiles_per_seq = seq // tm
    return pl.pallas_call(
        functools.partial(_even_proj_kernel, rope=rope, q_scale=HEAD_DIM ** -0.5 * LOG2E),
        grid=(nt // tm, n_cols),
        in_specs=[
            pl.BlockSpec((tm, d), lambda i, j: (i, 0)),
            pl.BlockSpec((1, d), lambda i, j: (0, 0)),
            pl.BlockSpec((d, tn), lambda i, j: (0, col0 + j)),
            pl.BlockSpec((tm, LANES), lambda i, j: (i % tiles_per_seq, 0)),
            pl.BlockSpec((tm, LANES), lambda i, j: (i % tiles_per_seq, 0)),
        ],
        out_specs=pl.BlockSpec((tm, tn), lambda i, j: (i, j)),
        out_shape=jax.ShapeDtypeStruct((nt, n_cols * tn), BF16),
        scratch_shapes=[pltpu.VMEM((tm, d), BF16)],
        compiler_params=_params("parallel", "arbitrary"),
        name="even_proj_qk" if rope else "even_proj_rest",
    )(h, g, w_in, cos128, sin128)


def _store_scores(q, k_ref, tail_bias, s_ref):
    s = _nt_dot(q, k_ref[...])
    n_main = s.shape[1] - LANES
    s_ref[:, :n_main] = s[:, :n_main]
    s_ref[:, n_main:] = s[:, n_main:] + tail_bias


def _probs(s_ref):
    s = s_ref[...]
    p = jnp.exp2(s - jnp.max(s, axis=-1, keepdims=True))
    return p, jnp.sum(p, axis=-1, keepdims=True)


def _tail_bias(seq, lp):
    lane = lax.broadcasted_iota(jnp.int32, (1, LANES), 1)
    return jnp.where(lane < seq - (lp - LANES), 0.0, NEG_BIG).astype(F32)


def _nt_dot(a, b):
    return lax.dot_general(a, b, (((1,), (1,)), ((), ())), preferred_element_type=F32)


def _attention_pipeline(n_tiles, scores, softmax, pv):
    sums = {}
    for step in range(n_tiles + 2):
        if 0 <= step - 2 < n_tiles:
            pv(step - 2, (step - 2) % 2, sums.pop(step - 2))
        if 0 <= step - 1 < n_tiles:
            sums[step - 1] = softmax((step - 1) % 2)
        if step < n_tiles:
            scores(step, step % 2)


def _diff_attn_kernel(lq1_ref, lk1_ref, lq2_ref, lk2_ref, q_ref, k_ref, v_ref, g_ref, o_ref, kpad, vpad,
                      s1a, s1b, s2a, s2b, pa, pb, *, seq, lam_init, tq):
    lp = kpad.shape[0]
    kpad[0:seq, :] = k_ref[0]
    kpad[seq:lp, :] = jnp.zeros((lp - seq, LANES), BF16)
    vpad[0:seq, :] = v_ref[0]
    vpad[seq:lp, :] = jnp.zeros((lp - seq, LANES), BF16)
    lam = (jnp.exp(jnp.sum(lq1_ref[...] * lk1_ref[...], axis=-1, keepdims=True))
           - jnp.exp(jnp.sum(lq2_ref[...] * lk2_ref[...], axis=-1, keepdims=True)) + lam_init)
    tail_bias = _tail_bias(seq, lp)
    gain = g_ref[...] * (1.0 - lam_init)
    s_bufs, p_bufs = ((s1a, s2a), (s1b, s2b)), (pa, pb)

    def scores(t, par):
        q = q_ref[0, t * tq:(t + 1) * tq, :]
        lane = lax.broadcasted_iota(jnp.int32, q.shape, 1)
        zero = jnp.zeros_like(q)
        _store_scores(jnp.where(lane < HEAD_DIM, q, zero), kpad, tail_bias, s_bufs[par][0])
        _store_scores(jnp.where(lane >= HEAD_DIM, q, zero), kpad, tail_bias, s_bufs[par][1])

    def softmax(par):
        p1, l1 = _probs(s_bufs[par][0])
        p2, l2 = _probs(s_bufs[par][1])
        p_bufs[par][...] = (p1 * (1.0 / l1) - p2 * (lam / l2)).astype(BF16)

    def pv(t, par, _):
        o = jnp.dot(p_bufs[par][...], vpad[...], preferred_element_type=F32)
        o = o * lax.rsqrt(jnp.mean(o * o, axis=-1, keepdims=True) + EPS) * gain
        o_ref[0, t * tq:(t + 1) * tq, :] = o.astype(o_ref.dtype)

    _attention_pipeline(seq // tq, scores, softmax, pv)


def _diff_attn(qk3, rest3, lq1, lk1, lq2, lk2, subln_g, lam_init, heads):
    nb, seq, _ = qk3.shape
    lp = pl.cdiv(seq, LANES) * LANES
    assert lp > seq
    tq = _tile(seq, ATTN_Q_TILE_MAX, 16)
    vec = lambda: pl.BlockSpec((1, HEAD_DIM), lambda b, h: (0, 0))
    return pl.pallas_call(
        functools.partial(_diff_attn_kernel, seq=seq, lam_init=lam_init, tq=tq),
        grid=(nb, heads),
        in_specs=[
            vec(), vec(), vec(), vec(),
            pl.BlockSpec((1, seq, LANES), lambda b, h: (b, 0, h)),
            pl.BlockSpec((1, seq, LANES), lambda b, h: (b, 0, heads + h)),
            pl.BlockSpec((1, seq, LANES), lambda b, h: (b, 0, h)),
            pl.BlockSpec((1, LANES), lambda b, h: (0, 0)),
        ],
        out_specs=pl.BlockSpec((1, seq, LANES), lambda b, h: (b, 0, h)),
        out_shape=jax.ShapeDtypeStruct((nb, seq, heads * LANES), BF16),
        scratch_shapes=[pltpu.VMEM((lp, LANES), BF16), pltpu.VMEM((lp, LANES), BF16)]
        + [pltpu.VMEM((tq, lp), F32)] * 4 + [pltpu.VMEM((tq, lp), BF16)] * 2,
        compiler_params=_params("parallel", "parallel"),
        name="diff_attn",
    )(lq1, lk1, lq2, lk2, qk3, qk3, rest3, subln_g)


def _conv_kernel(gb_ref, gc_ref, cx_ref, w_ref, o_ref):
    u = gc_ref[0].astype(F32) * cx_ref[0].astype(F32)
    seq = u.shape[0]
    row = lax.broadcasted_iota(jnp.int32, u.shape, 0)
    prev = jnp.where(row == 0, 0.0, pltpu.roll(u, 1, 0))
    nxt = jnp.where(row == seq - 1, 0.0, pltpu.roll(u, seq - 1, 0))
    y = prev * w_ref[0:1, :] + u * w_ref[1:2, :] + nxt * w_ref[2:3, :]
    o_ref[0] = (gb_ref[0].astype(F32) * y).astype(o_ref.dtype)


def _gated_conv(rest3, conv_w, width):
    nb, seq, _ = rest3.shape
    tc = _tile(width, 256, LANES)
    nc = width // tc
    blk = lambda part: pl.BlockSpec((1, seq, tc), lambda b, c: (b, 0, part * nc + c))
    return pl.pallas_call(
        _conv_kernel,
        grid=(nb, nc),
        in_specs=[blk(1), blk(2), blk(3), pl.BlockSpec((3, tc), lambda b, c: (0, c))],
        out_specs=pl.BlockSpec((1, seq, tc), lambda b, c: (b, 0, c)),
        out_shape=jax.ShapeDtypeStruct((nb, seq, width), BF16),
        compiler_params=_params("parallel", "parallel"),
        name="gated_conv",
    )(rest3, rest3, rest3, conv_w)


def _out_proj_router_kernel(a1_ref, a2_ref, w_ref, h_ref, g_ref, wr_ref, hn_ref, tn_ref, aff_ref):
    k1 = a1_ref.shape[1]
    y = jnp.dot(a1_ref[...], w_ref[0:k1, :], preferred_element_type=F32)
    y = y + jnp.dot(a2_ref[...], w_ref[k1:, :], preferred_element_type=F32)
    h = h_ref[...] + y
    hn_ref[...] = h
    t = _rms(h, g_ref[...])
    d = t.shape[1]
    n_exp = aff_ref.shape[1]
    logits = jnp.dot(t.astype(BF16), wr_ref[...], preferred_element_type=F32)
    lane = lax.broadcasted_iota(jnp.int32, logits.shape, 1)
    logits = jnp.where(lane < n_exp, logits, NEG_BIG)
    e = jnp.exp(logits - jnp.max(logits, axis=-1, keepdims=True))
    aff = e / jnp.sum(e, axis=-1, keepdims=True)
    aff_ref[...] = aff[:, :n_exp]
    tn_ref[:, :d] = t
    tn_ref[:, d:] = aff


def _out_proj_router(a1, a2, col1, col2, w_out, h, ffn_g, w_router):
    nt, d = h.shape
    k1 = w_out.shape[0] // 2
    n_exp = w_router.shape[1]
    w_router = jnp.pad(w_router, ((0, 0), (0, LANES - n_exp)))
    tm = _tile(nt, 512, 16)
    row = lambda c: (lambda i: (i, c))
    const = lambda i: (0, 0)
    return pl.pallas_call(
        _out_proj_router_kernel,
        grid=(nt // tm,),
        in_specs=[
            pl.BlockSpec((tm, k1), row(col1)),
            pl.BlockSpec((tm, k1), row(col2)),
            pl.BlockSpec(w_out.shape, const),
            pl.BlockSpec((tm, d), row(0)),
            pl.BlockSpec((1, d), const),
            pl.BlockSpec(w_router.shape, const),
        ],
        out_specs=[
            pl.BlockSpec((tm, d), row(0)),
            pl.BlockSpec((tm, d + LANES), row(0)),
            pl.BlockSpec((tm, n_exp), row(0)),
        ],
        out_shape=[
            jax.ShapeDtypeStruct((nt, d), F32),
            jax.ShapeDtypeStruct((nt, d + LANES), F32),
            jax.ShapeDtypeStruct((nt, n_exp), F32),
        ],
        compiler_params=_params("parallel"),
        name="out_proj_router",
    )(a1, a2, w_out, h, ffn_g, w_router)


def _odd_in_kernel(x_ref, g_ref, w_ref, qg_ref, kvg_ref, cos_ref, sin_ref, cq_ref, ckv_ref, kr_ref):
    xn = _rms(x_ref[...], g_ref[...]).astype(BF16)
    p = jnp.dot(xn, w_ref[...], preferred_element_type=F32)
    qr, kvr = cq_ref.shape[1], ckv_ref.shape[1]
    cq_ref[...] = _rms(p[:, :qr], qg_ref[...]).astype(cq_ref.dtype)
    ckv_ref[...] = _rms(p[:, qr: qr + kvr], kvg_ref[...]).astype(ckv_ref.dtype)
    kr_ref[...] = _rope128(p[:, qr + kvr:], cos_ref[...], sin_ref[...]).astype(kr_ref.dtype)


def _odd_in(h, g, w_in_pad, q_g, kv_g, cos128, sin128, seq):
    nt, d = h.shape
    qr, kvr = q_g.shape[1], kv_g.shape[1]
    tm = _tile(seq, 768, 16)
    tiles_per_seq = seq // tm
    row = lambda i: (i, 0)
    const = lambda i: (0, 0)
    pos = lambda i: (i % tiles_per_seq, 0)
    return pl.pallas_call(
        _odd_in_kernel,
        grid=(nt // tm,),
        in_specs=[
            pl.BlockSpec((tm, d), row),
            pl.BlockSpec((1, d), const),
            pl.BlockSpec(w_in_pad.shape, const),
            pl.BlockSpec((1, qr), const),
            pl.BlockSpec((1, kvr), const),
            pl.BlockSpec((tm, LANES), pos),
            pl.BlockSpec((tm, LANES), pos),
        ],
        out_specs=[
            pl.BlockSpec((tm, qr), row),
            pl.BlockSpec((tm, kvr), row),
            pl.BlockSpec((tm, LANES), row),
        ],
        out_shape=[
            jax.ShapeDtypeStruct((nt, qr), BF16),
            jax.ShapeDtypeStruct((nt, kvr), BF16),
            jax.ShapeDtypeStruct((nt, LANES), BF16),
        ],
        compiler_params=_params("parallel"),
        name="odd_in",
    )(h, g, w_in_pad, q_g, kv_g, cos128, sin128)


def _up_proj_kernel(a_ref, w_ref, cos_ref, sin_ref, o_ref, *, rope, scale):
    a = a_ref[...]
    cos, sin = cos_ref[...], sin_ref[...]
    for c in range(o_ref.shape[1] // MLA_QK_PAD):
        lo = c * MLA_QK_PAD
        acc = jnp.dot(a, w_ref[:, lo: lo + MLA_QK_PAD], preferred_element_type=F32)
        if rope:
            o_ref[:, lo: lo + LANES] = (acc[:, :LANES] * scale).astype(o_ref.dtype)
            o_ref[:, lo + LANES: lo + MLA_QK_PAD] = (_rope128(acc[:, LANES:], cos, sin) * scale).astype(o_ref.dtype)
        else:
            o_ref[:, lo: lo + MLA_QK_PAD] = acc.astype(o_ref.dtype)


def _up_proj(a, w, cos128, sin128, seq, rope, scale):
    nt, k = a.shape
    n = w.shape[1]
    tm = _tile(seq, 768, 16)
    tiles_per_seq = seq // tm
    pos = lambda i: (i % tiles_per_seq, 0)
    return pl.pallas_call(
        functools.partial(_up_proj_kernel, rope=rope, scale=scale),
        grid=(nt // tm,),
        in_specs=[
            pl.BlockSpec((tm, k), lambda i: (i, 0)),
            pl.BlockSpec((k, n), lambda i: (0, 0)),
            pl.BlockSpec((tm, LANES), pos),
            pl.BlockSpec((tm, LANES), pos),
        ],
        out_specs=pl.BlockSpec((tm, n), lambda i: (i, 0)),
        out_shape=jax.ShapeDtypeStruct((nt, n), BF16),
        compiler_params=_params("parallel"),
        name="q_up_proj" if rope else "kv_up_proj",
    )(a, w, cos128, sin128)


def _mla_attn_kernel(q_ref, kn_ref, kr_ref, v_ref, o_ref, kcat, vpad, sa, sb, pa, pb, *, seq, tq):
    lp = kcat.shape[0]
    kcat[0:seq, 0:LANES] = kn_ref[0]
    kcat[0:seq, LANES:] = kr_ref[0]
    kcat[seq:lp, :] = jnp.zeros((lp - seq, MLA_QK_PAD), BF16)
    vpad[0:seq, :] = v_ref[0]
    vpad[seq:lp, :] = jnp.zeros((lp - seq, LANES), BF16)
    tail_bias = _tail_bias(seq, lp)

    s_bufs, p_bufs = (sa, sb), (pa, pb)

    def scores(t, par):
        _store_scores(q_ref[0, t * tq:(t + 1) * tq, :], kcat, tail_bias, s_bufs[par])

    def softmax(par):
        p, l = _probs(s_bufs[par])
        p_bufs[par][...] = p.astype(BF16)
        return l

    def pv(t, par, l):
        o = jnp.dot(p_bufs[par][...], vpad[...], preferred_element_type=F32) / l
        o_ref[0, t * tq:(t + 1) * tq, :] = o.astype(o_ref.dtype)

    _attention_pipeline(seq // tq, scores, softmax, pv)


def _mla_attn(q3, kv3, kr3, heads):
    nb, seq, _ = q3.shape
    lp = pl.cdiv(seq, LANES) * LANES
    assert lp > seq
    tq = _tile(seq, ATTN_Q_TILE_MAX, 16)
    return pl.pallas_call(
        functools.partial(_mla_attn_kernel, seq=seq, tq=tq),
        grid=(nb, heads),
        in_specs=[
            pl.BlockSpec((1, seq, MLA_QK_PAD), lambda b, h: (b, 0, h)),
            pl.BlockSpec((1, seq, LANES), lambda b, h: (b, 0, h)),
            pl.BlockSpec((1, seq, LANES), lambda b, h: (b, 0, 0)),
            pl.BlockSpec((1, seq, LANES), lambda b, h: (b, 0, heads + h)),
        ],
        out_specs=pl.BlockSpec((1, seq, LANES), lambda b, h: (b, 0, h)),
        out_shape=jax.ShapeDtypeStruct((nb, seq, heads * LANES), BF16),
        scratch_shapes=[pltpu.VMEM((lp, MLA_QK_PAD), BF16), pltpu.VMEM((lp, LANES), BF16)]
        + [pltpu.VMEM((tq, lp), F32)] * 2 + [pltpu.VMEM((tq, lp), BF16)] * 2,
        compiler_params=_params("parallel", "parallel"),
        name="mla_attn",
    )(q3, kv3, kr3, kv3)


def _moe_ffn_kernel(idx_ref, tn_hbm, wg_ref, wu_ref, wd_ref, o_ref, xs_buf, sem, *, tm):
    step = pl.program_id(0) * pl.num_programs(1) + pl.program_id(1)
    n_steps = pl.num_programs(0) * pl.num_programs(1)
    slot = step % 2

    def gather(tile, into):
        for r in range(tm):
            tok = idx_ref[tile * tm + r]
            pltpu.make_async_copy(tn_hbm.at[pl.ds(tok, 1), :], xs_buf.at[into, pl.ds(r, 1), :], sem.at[into]).start()

    def wait(which):
        pltpu.make_async_copy(tn_hbm.at[pl.ds(0, tm), :], xs_buf.at[which], sem.at[which]).wait()

    @pl.when(step == 0)
    def _():
        gather(0, 0)

    wait(slot)
    d = o_ref.shape[1]
    xs = xs_buf[slot, :, :d].astype(BF16)
    aff = xs_buf[slot, :, d:]
    lane = lax.broadcasted_iota(jnp.int32, aff.shape, 1)
    gate = jnp.sum(jnp.where(lane == pl.program_id(0), aff, 0.0), axis=-1, keepdims=True)
    hg = jnp.dot(xs, wg_ref[...], preferred_element_type=F32)
    hu = jnp.dot(xs, wu_ref[...], preferred_element_type=F32)
    hid = (hg * jax.nn.sigmoid(hg) * hu).astype(BF16)
    gather(step + 1, 1 - slot)
    o_ref[...] = (jnp.dot(hid, wd_ref[...], preferred_element_type=F32) * gate).astype(o_ref.dtype)

    @pl.when(step == n_steps - 1)
    def _():
        wait(1 - slot)


def _moe_ffn(idx_flat, tn, w_gate, w_up, w_down, layer, slots):
    _, n_exp, d, ff = w_gate.shape
    tm = _tile(slots, FFN_ROWS_MAX, 16)
    tiles = slots // tm
    grid_spec = pltpu.PrefetchScalarGridSpec(
        num_scalar_prefetch=1,
        grid=(n_exp, tiles),
        in_specs=[
            pl.BlockSpec(memory_space=pl.ANY),
            pl.BlockSpec((None, None, d, ff), lambda e, m, idx: (layer, e, 0, 0)),
            pl.BlockSpec((None, None, d, ff), lambda e, m, idx: (layer, e, 0, 0)),
            pl.BlockSpec((None, None, ff, d), lambda e, m, idx: (layer, e, 0, 0)),
        ],
        out_specs=pl.BlockSpec((tm, d), lambda e, m, idx: (e * tiles + m, 0)),
        scratch_shapes=[pltpu.VMEM((2, tm, tn.shape[1]), F32), pltpu.SemaphoreType.DMA((2,))],
    )
    idx_padded = jnp.concatenate([idx_flat, idx_flat[:tm]])
    return pl.pallas_call(
        functools.partial(_moe_ffn_kernel, tm=tm),
        grid_spec=grid_spec,
        out_shape=jax.ShapeDtypeStruct((n_exp * slots, d), BF16),
        compiler_params=_params("arbitrary", "arbitrary"),
        name="moe_ffn",
    )(idx_padded, tn, w_gate, w_up, w_down)


def _combine_kernel(src_ref, pieces_ref, dst_ref, chunks_ref, ye_hbm, h_ref, tgt_ref, o_ref, stage, tgt_b, sem,
                    *, static_chunks):
    i = pl.program_id(0)
    n_tiles = pl.num_programs(0)
    n_exp = tgt_ref.shape[1]
    slot = i % 2

    @pl.when(i == 0)
    def _():
        stage[...] = jnp.zeros(stage.shape, stage.dtype)

    def piece(src_row, into, dst_row):
        return pltpu.make_async_copy(ye_hbm.at[pl.ds(src_row, COMBINE_PIECE), :],
                                     stage.at[into, pl.ds(dst_row, COMBINE_PIECE), :], sem.at[into])

    def fetch(tile, into, live):
        n_pieces = 0
        for e in range(n_exp):
            src, dst = src_ref[tile * n_exp + e], dst_ref[tile * n_exp + e]
            n = pieces_ref[tile * n_exp + e] * live

            def issue(p, carry, src=src, dst=dst):
                piece(pl.multiple_of(src + p * COMBINE_PIECE, COMBINE_PIECE), into,
                      pl.multiple_of(dst + p * COMBINE_PIECE, COMBINE_PIECE)).start()
                return carry

            lax.fori_loop(0, n, issue, 0)
            n_pieces = n_pieces + n
        return n_pieces

    @pl.when(i == 0)
    def _():
        fetch(0, 0, 1)

    n_mine = 0
    for e in range(n_exp):
        n_mine = n_mine + pieces_ref[i * n_exp + e]

    def wait(p, carry):
        piece(0, slot, 0).wait()
        return carry

    lax.fori_loop(0, n_mine, wait, 0)
    nxt = jnp.minimum(i + 1, n_tiles - 1)
    fetch(nxt, 1 - slot, jnp.where(i + 1 < n_tiles, 1, 0))
    t_tile = tgt_ref.shape[0]
    for e in range(n_exp):
        tgt_b[e] = jnp.broadcast_to(tgt_ref[:, e:e + 1], (t_tile, LANES))
    lane = lax.broadcasted_iota(jnp.int32, (1, LANES), 1)

    def onehot(base, n_cols):
        parts = []
        for part in range(n_cols // LANES):
            want = lane + (base + part * LANES)
            hit = jnp.zeros((t_tile, LANES), F32)
            for e in range(n_exp):
                hit = jnp.where(tgt_b[e] == want, 1.0, hit)
            parts.append(hit.astype(BF16))
        return jnp.concatenate(parts, axis=1)

    k0 = static_chunks * COMBINE_CHUNK
    o_ref[...] = h_ref[...] + jnp.dot(onehot(0, k0), stage[slot, 0:k0, :], preferred_element_type=F32)

    def chunk(c, carry):
        base = pl.multiple_of(c * COMBINE_CHUNK, COMBINE_CHUNK)
        o_ref[...] += jnp.dot(onehot(base, COMBINE_CHUNK), stage[slot, pl.ds(base, COMBINE_CHUNK), :],
                              preferred_element_type=F32)
        return carry

    lax.fori_loop(static_chunks, chunks_ref[i], chunk, 0)


def _combine(ye, h, sel_t, row_t, t_tile):
    nt, d = h.shape
    n_exp = sel_t.shape[1]
    assert ye.shape[0] % COMBINE_PIECE == 0
    first = row_t[::t_tile]
    last = (row_t + sel_t)[t_tile - 1::t_tile]
    src = first // COMBINE_PIECE * COMBINE_PIECE
    pieces = (last - src + COMBINE_PIECE - 1) // COMBINE_PIECE
    dst = (jnp.cumsum(pieces, axis=1) - pieces) * COMBINE_PIECE
    chunks = (jnp.sum(pieces, axis=1) * COMBINE_PIECE + COMBINE_CHUNK - 1) // COMBINE_CHUNK
    tgt = jnp.where(sel_t > 0, row_t - jnp.repeat(src - dst, t_tile, axis=0), -1)
    stage_rows = pl.cdiv(n_exp * (t_tile + 2 * (COMBINE_PIECE - 1)), COMBINE_CHUNK) * COMBINE_CHUNK
    usual_rows = n_exp * (N_EXPERT_CAPACITY_FACTOR * t_tile // n_exp + COMBINE_PIECE - 1)
    static_chunks = min(pl.cdiv(usual_rows, COMBINE_CHUNK), stage_rows // COMBINE_CHUNK)
    grid_spec = pltpu.PrefetchScalarGridSpec(
        num_scalar_prefetch=4,
        grid=(nt // t_tile,),
        in_specs=[
            pl.BlockSpec(memory_space=pl.ANY),
            pl.BlockSpec((t_tile, d), lambda i, *_: (i, 0)),
            pl.BlockSpec((t_tile, n_exp), lambda i, *_: (i, 0)),
        ],
        out_specs=pl.BlockSpec((t_tile, d), lambda i, *_: (i, 0)),
        scratch_shapes=[pltpu.VMEM((2, stage_rows, d), ye.dtype), pltpu.VMEM((n_exp, t_tile, LANES), jnp.int32),
                        pltpu.SemaphoreType.DMA((2,))],
    )
    flat = lambda x: x.reshape(-1).astype(jnp.int32)
    return pl.pallas_call(
        functools.partial(_combine_kernel, static_chunks=static_chunks),
        grid_spec=grid_spec,
        out_shape=jax.ShapeDtypeStruct((nt, d), F32),
        compiler_params=_params("arbitrary"),
        name="moe_combine",
    )(flat(src), flat(pieces), flat(dst), flat(chunks), ye, h, tgt.astype(jnp.int32))


def _threshold_kernel(x_ref, tau_ref, *, cap):
    n_exp = x_ref.shape[0]
    cap_f = float(cap)

    def bisect(_, bounds):
        new = []
        for e in range(n_exp):
            lo, hi = bounds[e]
            mid = lo + ((hi - lo) >> 1)
            above = jnp.where(pltpu.bitcast(x_ref[e], jnp.int32) >= mid, 1.0, 0.0)
            ok = jnp.sum(above, axis=(0, 1), keepdims=True) >= cap_f
            new.append((jnp.where(ok, mid, lo), jnp.where(ok, hi, mid)))
        return tuple(new)

    start = (jnp.zeros((1, 1), jnp.int32), jnp.full((1, 1), F32_INF_BITS, jnp.int32))
    bounds = lax.fori_loop(0, 31, bisect, tuple(start for _ in range(n_exp)))
    for e in range(n_exp):
        tau_ref[e] = jnp.broadcast_to(bounds[e][0], tau_ref.shape[1:])


def _route_kernel(x_ref, tau_ref, sel_ref, pexc_ref, idx_ref, *, cap, slot_block):
    x = x_ref[0]
    n_rows = x.shape[0]
    bits = pltpu.bitcast(x, jnp.int32)
    cap_f = float(cap)
    tau = tau_ref[0, 0:1, 0:1]

    upper = jnp.where(lax.broadcasted_iota(jnp.int32, (LANES, LANES), 0)
                      <= lax.broadcasted_iota(jnp.int32, (LANES, LANES), 1), 1.0, 0.0).astype(BF16)
    lower = jnp.where(lax.broadcasted_iota(jnp.int32, (n_rows, n_rows), 1)
                      < lax.broadcasted_iota(jnp.int32, (n_rows, n_rows), 0), 1.0, 0.0).astype(BF16)

    def prefix(m):
        within = jnp.dot(m.astype(BF16), upper, preferred_element_type=F32)
        row_tot = jnp.broadcast_to(within[:, LANES - 1:], within.shape)
        before = jnp.dot(lower, row_tot.astype(BF16), preferred_element_type=F32)
        return within, before, row_tot

    gt = jnp.where(bits > tau, 1.0, 0.0)
    eq = jnp.where(bits == tau, 1.0, 0.0)
    need = cap_f - jnp.sum(gt, axis=(0, 1), keepdims=True)
    tie_within, tie_before, _ = prefix(eq)
    tie_rank = tie_within + tie_before - eq
    sel = gt + eq * jnp.where(tie_rank < need, 1.0, 0.0)
    within, before, row_tot = prefix(sel)
    sel_ref[0] = sel.astype(jnp.int32)
    pexc_ref[0] = (within + before - sel).astype(jnp.int32)

    row_end = jnp.transpose(before + row_tot)[0:1, :]
    row_start = jnp.transpose(before)[0:1, :]
    within_b = within.astype(BF16)
    row_id = lax.broadcasted_iota(jnp.int32, (1, n_rows), 1).astype(F32)
    for s in range(idx_ref.shape[1] // slot_block):
        j = (lax.broadcasted_iota(jnp.int32, (slot_block, 1), 0) + s * slot_block).astype(F32)
        row = jnp.sum(jnp.where(row_end <= j, 1.0, 0.0), axis=-1, keepdims=True)
        onehot = jnp.where(row_id == row, 1.0, 0.0)
        cum = jnp.dot(onehot.astype(BF16), within_b, preferred_element_type=F32)
        local = j - jnp.sum(onehot * row_start, axis=-1, keepdims=True)
        col = jnp.sum(jnp.where(cum <= local, 1.0, 0.0), axis=-1, keepdims=True)
        tok = (row * LANES + col).astype(jnp.int32)
        idx_ref[0, s * slot_block:(s + 1) * slot_block, :] = jnp.where(j < cap_f, tok, 0)


def _route(aff_rows, cap):
    n_exp, n_rows, _ = aff_rows.shape
    cap_pad = pl.cdiv(cap, ROUTE_SLOT_BLOCK) * ROUTE_SLOT_BLOCK
    blk = pl.BlockSpec((1, n_rows, LANES), lambda e: (e, 0, 0))
    tau = pl.pallas_call(
        functools.partial(_threshold_kernel, cap=cap),
        out_shape=jax.ShapeDtypeStruct((n_exp, 8, LANES), jnp.int32),
        compiler_params=pltpu.CompilerParams(vmem_limit_bytes=VMEM_LIMIT_BYTES),
        name="route_threshold",
    )(aff_rows)
    return pl.pallas_call(
        functools.partial(_route_kernel, cap=cap, slot_block=ROUTE_SLOT_BLOCK),
        grid=(n_exp,),
        in_specs=[blk, pl.BlockSpec((1, 8, LANES), lambda e: (e, 0, 0))],
        out_specs=[blk, blk, pl.BlockSpec((1, cap_pad, 1), lambda e: (e, 0, 0))],
        out_shape=[
            jax.ShapeDtypeStruct(aff_rows.shape, jnp.int32),
            jax.ShapeDtypeStruct(aff_rows.shape, jnp.int32),
            jax.ShapeDtypeStruct((n_exp, cap_pad, 1), jnp.int32),
        ],
        compiler_params=_params("parallel"),
        name="route",
    )(aff_rows, tau)


def _moe(h, tn, aff, group_sizes, w_gate, w_up, w_down, layer):
    n_exp = aff.shape[1]
    caps = [N_EXPERT_CAPACITY_FACTOR * n // n_exp for n in group_sizes]
    slots = pl.cdiv(sum(caps), FFN_SLOT_ALIGN) * FFN_SLOT_ALIGN
    idx_parts, sel_parts, row_parts = [], [], []
    tok0, slot0 = 0, 0
    for n, cap in zip(group_sizes, caps):
        n_rows = pl.cdiv(pl.cdiv(n, LANES), LANES) * LANES
        rows = jnp.pad(aff[tok0: tok0 + n].T, ((0, 0), (0, n_rows * LANES - n)), constant_values=-1.0)
        sel, pexc, idx = _route(rows.reshape(n_exp, n_rows, LANES), cap)
        idx_parts.append(idx[:, :cap, 0] + tok0)
        sel_parts.append(sel.reshape(n_exp, -1)[:, :n].T)
        expert_row0 = jnp.arange(n_exp, dtype=jnp.int32)[None, :] * slots + slot0
        row_parts.append(pexc.reshape(n_exp, -1)[:, :n].T + expert_row0)
        tok0 += n
        slot0 += cap
    idx_parts.append(jnp.zeros((n_exp, slots - sum(caps)), jnp.int32))
    idx_all = jnp.concatenate(idx_parts, axis=1)
    sel_t = jnp.concatenate(sel_parts, axis=0)
    row_t = jnp.concatenate(row_parts, axis=0)
    ye = _moe_ffn(idx_all.reshape(-1), tn, w_gate, w_up, w_down, layer, slots)
    t_tile = _tile(math.gcd(*group_sizes), COMBINE_TOKENS_MAX, 8)
    return _combine(ye, h, sel_t, row_t, t_tile)


def _final_norm_kernel(x_ref, g_ref, o_ref):
    o_ref[...] = _rms(x_ref[...], g_ref[...])


def _final_norm(h, g, first_seq, n_seqs, seq):
    _, d = h.shape
    s_out = seq - N_META
    tm = _tile(s_out, 512, 8)
    per_seq = s_out // tm
    out = pl.pallas_call(
        _final_norm_kernel,
        grid=(n_seqs, per_seq),
        in_specs=[
            pl.BlockSpec((pl.Element(tm), pl.Element(d)),
                         lambda b, j: (pl.multiple_of((first_seq + b) * seq + N_META + j * tm, 8), 0)),
            pl.BlockSpec((1, d), lambda b, j: (0, 0)),
        ],
        out_specs=pl.BlockSpec((tm, d), lambda b, j: (b * per_seq + j, 0)),
        out_shape=jax.ShapeDtypeStruct((n_seqs * s_out, d), F32),
        compiler_params=_params("parallel", "parallel"),
        name="final_norm",
    )(h, g)
    return out.reshape(n_seqs, s_out, d)


def _lambda_init(layer):
    return 0.8 - 0.6 * math.exp(-0.3 * layer)


def kernel(x_prompt, x_sample, meta_tokens, mix_norm_g, ffn_norm_g, final_norm_g, ev_w_in, ev_w_out, ev_lambda_q1, ev_lambda_k1, ev_lambda_q2, ev_lambda_k2, ev_subln_g, ev_conv_w, od_w_in, od_q_norm_g, od_w_q_b, od_kv_norm_g, od_w_kv_b, od_w_out, moe_w_router, moe_w_gate, moe_w_up, moe_w_down):
    d = x_prompt.shape[-1]
    seq = N_META + x_prompt.shape[1]
    assert x_sample.shape[1] == x_prompt.shape[1]
    nb_p, nb_s = x_prompt.shape[0], x_sample.shape[0]
    nb = nb_p + nb_s
    nt = nb * seq
    group_sizes = (nb_p * seq, nb_s * seq)
    depth = mix_norm_g.shape[0]

    x = jnp.concatenate([x_prompt, x_sample], axis=0)
    meta = jnp.broadcast_to(meta_tokens[None].astype(x.dtype), (nb, N_META, d))
    h = jnp.concatenate([meta, x], axis=1).reshape(nt, d)
    cos128, sin128 = _rope_tables(seq)
    w_gate, w_up, w_down = moe_w_gate.astype(BF16), moe_w_up.astype(BF16), moe_w_down.astype(BF16)

    for layer in range(depth):
        i = layer // 2
        mix_g = mix_norm_g[layer][None]
        ffn_g = ffn_norm_g[layer][None]
        w_router = moe_w_router[layer].astype(BF16)
        if layer % 2 == 0:
            width = d // 2
            heads = width // LANES
            w_in = ev_w_in[i].astype(BF16)
            qk3 = _even_proj(h, mix_g, w_in, cos128, sin128, seq, True).reshape(nb, seq, 2 * width)
            rest3 = _even_proj(h, mix_g, w_in, cos128, sin128, seq, False).reshape(nb, seq, 4 * width)
            attn = _diff_attn(qk3, rest3, ev_lambda_q1[i][None], ev_lambda_k1[i][None], ev_lambda_q2[i][None],
                              ev_lambda_k2[i][None], ev_subln_g[i][None], _lambda_init(layer), heads)
            conv = _gated_conv(rest3, ev_conv_w[i], width)
            h, tn, aff = _out_proj_router(attn.reshape(nt, width), conv.reshape(nt, width), 0, 0,
                                          ev_w_out[i].astype(BF16), h, ffn_g, w_router)
        else:
            q_rank, kv_rank = od_q_norm_g.shape[1], od_kv_norm_g.shape[1]
            heads = od_w_q_b.shape[2] // (MLA_NOPE + MLA_ROPE)
            w_in_pad = jnp.pad(od_w_in[i], ((0, 0), (0, LANES - MLA_ROPE))).astype(BF16)
            w_q = od_w_q_b[i].reshape(q_rank, heads, MLA_NOPE + MLA_ROPE)
            w_q = jnp.pad(w_q, ((0, 0), (0, 0), (0, MLA_QK_PAD - MLA_NOPE - MLA_ROPE)))
            w_q = w_q.reshape(q_rank, heads * MLA_QK_PAD).astype(BF16)
            w_kv = od_w_kv_b[i].reshape(kv_rank, heads, MLA_NOPE + MLA_V)
            w_kv = jnp.concatenate([w_kv[:, :, :MLA_NOPE].reshape(kv_rank, heads * MLA_NOPE),
                                    w_kv[:, :, MLA_NOPE:].reshape(kv_rank, heads * MLA_V)], axis=1).astype(BF16)
            cq, ckv, kr = _odd_in(h, mix_g, w_in_pad, od_q_norm_g[i][None], od_kv_norm_g[i][None], cos128, sin128, seq)
            q = _up_proj(cq, w_q, cos128, sin128, seq, True, (MLA_NOPE + MLA_ROPE) ** -0.5 * LOG2E)
            kv = _up_proj(ckv, w_kv, cos128, sin128, seq, False, 1.0)
            o = _mla_attn(q.reshape(nb, seq, heads * MLA_QK_PAD), kv.reshape(nb, seq, 2 * heads * LANES),
                          kr.reshape(nb, seq, LANES), heads)
            o = o.reshape(nt, heads * MLA_V)
            h, tn, aff = _out_proj_router(o, o, 0, 1, od_w_out[i].astype(BF16), h, ffn_g, w_router)
        h = _moe(h, tn, aff, group_sizes, w_gate, w_up, w_down, layer)

    g = final_norm_g[None]
    return (_final_norm(h, g, 0, nb_p, seq), _final_norm(h, g, nb_p, nb_s, seq))
```

```python
import functools
import math

import jax
import jax.numpy as jnp
from jax import lax
from jax.experimental import pallas as pl
from jax.experimental.pallas import tpu as pltpu

N_META = 16
EPS = 1e-6
ROPE_THETA = 10000.0
HEAD_DIM = 64
LANES = 128
MXU_COLS = 256
MLA_NOPE = 128
MLA_ROPE = 64
MLA_V = 128
MLA_QK_PAD = 256
ATTN_Q_TILE_MAX = 768
ATTN_ROW_BLOCK = 16
MLA_HEADS_PER_STEP = 2
DIFF_HEADS_PER_STEP = 1
N_EXPERT_CAPACITY_FACTOR = 2
VMEM_LIMIT_BYTES = 56 * 1024 * 1024
NEG_BIG = -1e30
F32_INF_BITS = 0x7F800000
LOG2E = math.log2(math.e)
ROUTE_SLOT_BLOCK = 512
COMBINE_TOKENS_MAX = 192
COMBINE_PIECE = 16
FFN_ROWS_MAX = 448
FFN_SLOT_ALIGN = 64
COMBINE_CHUNK = 256

F32 = jnp.float32
BF16 = jnp.bfloat16


def _tile(n, cap, mult):
    best = None
    for t in range(mult, min(n, cap) + 1, mult):
        if n % t == 0:
            best = t
    assert best is not None, (n, cap, mult)
    return best


def _params(*sem):
    return pltpu.CompilerParams(dimension_semantics=sem, vmem_limit_bytes=VMEM_LIMIT_BYTES)


def _rms(x, g):
    return x * lax.rsqrt(jnp.mean(x * x, axis=-1, keepdims=True) + EPS) * g


def _rope128(x, cos, sin_signed):
    lane = lax.broadcasted_iota(jnp.int32, x.shape, 1)
    first_half = (lane % HEAD_DIM) < (HEAD_DIM // 2)
    rot = jnp.where(first_half, pltpu.roll(x, LANES - HEAD_DIM // 2, 1), pltpu.roll(x, HEAD_DIM // 2, 1))
    return x * cos + rot * sin_signed


def _rope_tables(length):
    inv = 1.0 / (ROPE_THETA ** (jnp.arange(0, HEAD_DIM, 2, dtype=F32) / HEAD_DIM))
    ang = jnp.arange(length, dtype=F32)[:, None] * inv[None, :]
    cos, sin = jnp.cos(ang), jnp.sin(ang)
    cos128 = jnp.tile(cos, (1, LANES // (HEAD_DIM // 2)))
    sin128 = jnp.tile(jnp.concatenate([-sin, sin], axis=1), (1, LANES // HEAD_DIM))
    return cos128, sin128


def _even_proj_kernel(x_ref, g_ref, w_ref, cos_ref, sin_ref, o_ref, xn_ref, *, rope, q_scale):
    j = pl.program_id(1)

    @pl.when(j == 0)
    def _():
        xn_ref[...] = _rms(x_ref[...], g_ref[...]).astype(BF16)

    xn = xn_ref[...]
    if rope:
        scale = jnp.where(j == 0, q_scale, 1.0).astype(F32)
        cos, sin = cos_ref[...], sin_ref[...]
    for c in range(o_ref.shape[1] // MXU_COLS):
        lo = c * MXU_COLS
        acc = jnp.dot(xn, w_ref[:, lo: lo + MXU_COLS], preferred_element_type=F32)
        if rope:
            for half in range(MXU_COLS // LANES):
                sl = slice(half * LANES, (half + 1) * LANES)
                o_ref[:, lo + half * LANES: lo + (half + 1) * LANES] = (
                    _rope128(acc[:, sl], cos, sin) * scale).astype(o_ref.dtype)
        else:
            o_ref[:, lo: lo + MXU_COLS] = acc.astype(o_ref.dtype)


def _even_proj(h, g, w_in, cos128, sin128, seq, rope):
    nt, d = h.shape
    tn = w_in.shape[1] // 6
    col0, n_cols = (0, 2) if rope else (2, 4)
    tm = _tile(seq, 768, 16)
    tiles_per_seq = seq // tm
    return pl.pallas_call(
        functools.partial(_even_proj_kernel, rope=rope, q_scale=HEAD_DIM ** -0.5 * LOG2E),
        grid=(nt // tm, n_cols),
        in_specs=[
            pl.BlockSpec((tm, d), lambda i, j: (i, 0)),
            pl.BlockSpec((1, d), lambda i, j: (0, 0)),
            pl.BlockSpec((d, tn), lambda i, j: (0, col0 + j)),
            pl.BlockSpec((tm, LANES), lambda i, j: (i % tiles_per_seq, 0)),
            pl.BlockSpec((tm, LANES), lambda i, j: (i % tiles_per_seq, 0)),
        ],
        out_specs=pl.BlockSpec((tm, tn), lambda i, j: (i, j)),
        out_shape=jax.ShapeDtypeStruct((nt, n_cols * tn), BF16),
        scratch_shapes=[pltpu.VMEM((tm, d), BF16)],
        compiler_params=_params("parallel", "arbitrary"),
        name="even_proj_qk" if rope else "even_proj_rest",
    )(h, g, w_in, cos128, sin128)


def _store_scores(q, k_ref, tail_bias, s_ref):
    s = _nt_dot(q, k_ref[...])
    n_main = s.shape[1] - LANES
    s_ref[:, :n_main] = s[:, :n_main]
    s_ref[:, n_main:] = s[:, n_main:] + tail_bias


def _probs(s):
    p = jnp.exp2(s - jnp.max(s, axis=-1, keepdims=True))
    return p, jnp.sum(p, axis=-1, keepdims=True)


def _row_blocks(n_rows):
    return [slice(r, r + ATTN_ROW_BLOCK) for r in range(0, n_rows, ATTN_ROW_BLOCK)]


def _tail_bias(seq, lp):
    lane = lax.broadcasted_iota(jnp.int32, (1, LANES), 1)
    return jnp.where(lane < seq - (lp - LANES), 0.0, NEG_BIG).astype(F32)


def _nt_dot(a, b):
    return lax.dot_general(a, b, (((1,), (1,)), ((), ())), preferred_element_type=F32)


def _attention_pipeline(n_tiles, scores, softmax, pv):
    sums = {}
    for step in range(n_tiles + 2):
        if 0 <= step - 2 < n_tiles:
            pv(step - 2, (step - 2) % 2, sums.pop(step - 2))
        if 0 <= step - 1 < n_tiles:
            sums[step - 1] = softmax((step - 1) % 2)
        if step < n_tiles:
            scores(step, step % 2)


def _diff_attn_kernel(lq1_ref, lk1_ref, lq2_ref, lk2_ref, q_ref, k_ref, v_ref, g_ref, o_ref, kpad, vpad,
                      s1a, s1b, s2a, s2b, pa, pb, *, seq, lam_init, tq):
    n_heads, lp = kpad.shape[0], kpad.shape[1]
    for h in range(n_heads):
        cols = slice(h * LANES, (h + 1) * LANES)
        kpad[h, 0:seq, :] = k_ref[0, :, cols]
        kpad[h, seq:lp, :] = jnp.zeros((lp - seq, LANES), BF16)
        vpad[h, 0:seq, :] = v_ref[0, :, cols]
        vpad[h, seq:lp, :] = jnp.zeros((lp - seq, LANES), BF16)
    lam = (jnp.exp(jnp.sum(lq1_ref[...] * lk1_ref[...], axis=-1, keepdims=True))
           - jnp.exp(jnp.sum(lq2_ref[...] * lk2_ref[...], axis=-1, keepdims=True)) + lam_init)
    tail_bias = _tail_bias(seq, lp)
    gain = g_ref[...] * (1.0 - lam_init)
    per_head = seq // tq
    s_bufs, p_bufs = ((s1a, s2a), (s1b, s2b)), (pa, pb)

    def scores(t, par):
        h, r = divmod(t, per_head)
        q = q_ref[0, r * tq:(r + 1) * tq, h * LANES:(h + 1) * LANES]
        lane = lax.broadcasted_iota(jnp.int32, q.shape, 1)
        zero = jnp.zeros_like(q)
        _store_scores(jnp.where(lane < HEAD_DIM, q, zero), kpad.at[h], tail_bias, s_bufs[par][0])
        _store_scores(jnp.where(lane >= HEAD_DIM, q, zero), kpad.at[h], tail_bias, s_bufs[par][1])

    def softmax(par):
        for rows in _row_blocks(tq):
            p1, l1 = _probs(s_bufs[par][0][rows, :])
            p2, l2 = _probs(s_bufs[par][1][rows, :])
            p_bufs[par][rows, :] = (p1 * (1.0 / l1) - p2 * (lam / l2)).astype(BF16)

    def pv(t, par, _):
        h, r = divmod(t, per_head)
        o = jnp.dot(p_bufs[par][...], vpad[h], preferred_element_type=F32)
        o = o * lax.rsqrt(jnp.mean(o * o, axis=-1, keepdims=True) + EPS) * gain
        o_ref[0, r * tq:(r + 1) * tq, h * LANES:(h + 1) * LANES] = o.astype(o_ref.dtype)

    _attention_pipeline(n_heads * per_head, scores, softmax, pv)


def _diff_attn(qk3, rest3, lq1, lk1, lq2, lk2, subln_g, lam_init, heads):
    nb, seq, _ = qk3.shape
    lp = pl.cdiv(seq, LANES) * LANES
    assert lp > seq
    tq = _tile(seq, ATTN_Q_TILE_MAX, 16)
    vec = lambda: pl.BlockSpec((1, HEAD_DIM), lambda b, h: (0, 0))
    hs = DIFF_HEADS_PER_STEP
    assert heads % hs == 0
    return pl.pallas_call(
        functools.partial(_diff_attn_kernel, seq=seq, lam_init=lam_init, tq=tq),
        grid=(nb, heads // hs),
        in_specs=[
            vec(), vec(), vec(), vec(),
            pl.BlockSpec((1, seq, hs * LANES), lambda b, h: (b, 0, h)),
            pl.BlockSpec((1, seq, hs * LANES), lambda b, h: (b, 0, heads // hs + h)),
            pl.BlockSpec((1, seq, hs * LANES), lambda b, h: (b, 0, h)),
            pl.BlockSpec((1, LANES), lambda b, h: (0, 0)),
        ],
        out_specs=pl.BlockSpec((1, seq, hs * LANES), lambda b, h: (b, 0, h)),
        out_shape=jax.ShapeDtypeStruct((nb, seq, heads * LANES), BF16),
        scratch_shapes=[pltpu.VMEM((hs, lp, LANES), BF16), pltpu.VMEM((hs, lp, LANES), BF16)]
        + [pltpu.VMEM((tq, lp), F32)] * 4 + [pltpu.VMEM((tq, lp), BF16)] * 2,
        compiler_params=_params("parallel", "parallel"),
        name="diff_attn",
    )(lq1, lk1, lq2, lk2, qk3, qk3, rest3, subln_g)


def _conv_kernel(gb_ref, gc_ref, cx_ref, w_ref, o_ref):
    u = gc_ref[0].astype(F32) * cx_ref[0].astype(F32)
    seq = u.shape[0]
    row = lax.broadcasted_iota(jnp.int32, u.shape, 0)
    prev = jnp.where(row == 0, 0.0, pltpu.roll(u, 1, 0))
    nxt = jnp.where(row == seq - 1, 0.0, pltpu.roll(u, seq - 1, 0))
    y = prev * w_ref[0:1, :] + u * w_ref[1:2, :] + nxt * w_ref[2:3, :]
    o_ref[0] = (gb_ref[0].astype(F32) * y).astype(o_ref.dtype)


def _gated_conv(rest3, conv_w, width):
    nb, seq, _ = rest3.shape
    tc = _tile(width, 256, LANES)
    nc = width // tc
    blk = lambda part: pl.BlockSpec((1, seq, tc), lambda b, c: (b, 0, part * nc + c))
    return pl.pallas_call(
        _conv_kernel,
        grid=(nb, nc),
        in_specs=[blk(1), blk(2), blk(3), pl.BlockSpec((3, tc), lambda b, c: (0, c))],
        out_specs=pl.BlockSpec((1, seq, tc), lambda b, c: (b, 0, c)),
        out_shape=jax.ShapeDtypeStruct((nb, seq, width), BF16),
        compiler_params=_params("parallel", "parallel"),
        name="gated_conv",
    )(rest3, rest3, rest3, conv_w)


def _out_proj_router_kernel(a1_ref, a2_ref, w_ref, h_ref, g_ref, wr_ref, hn_ref, tn_ref, aff_ref):
    k1 = a1_ref.shape[1]
    y = jnp.dot(a1_ref[...], w_ref[0:k1, :], preferred_element_type=F32)
    y = y + jnp.dot(a2_ref[...], w_ref[k1:, :], preferred_element_type=F32)
    h = h_ref[...] + y
    hn_ref[...] = h
    t = _rms(h, g_ref[...])
    d = t.shape[1]
    n_exp = aff_ref.shape[1]
    logits = jnp.dot(t.astype(BF16), wr_ref[...], preferred_element_type=F32)
    lane = lax.broadcasted_iota(jnp.int32, logits.shape, 1)
    logits = jnp.where(lane < n_exp, logits, NEG_BIG)
    e = jnp.exp(logits - jnp.max(logits, axis=-1, keepdims=True))
    aff = e / jnp.sum(e, axis=-1, keepdims=True)
    aff_ref[...] = aff[:, :n_exp]
    tn_ref[:, :d] = t
    tn_ref[:, d:] = aff


def _out_proj_router(a1, a2, col1, col2, w_out, h, ffn_g, w_router):
    nt, d = h.shape
    k1 = w_out.shape[0] // 2
    n_exp = w_router.shape[1]
    w_router = jnp.pad(w_router, ((0, 0), (0, LANES - n_exp)))
    tm = _tile(nt, 512, 16)
    row = lambda c: (lambda i: (i, c))
    const = lambda i: (0, 0)
    return pl.pallas_call(
        _out_proj_router_kernel,
        grid=(nt // tm,),
        in_specs=[
            pl.BlockSpec((tm, k1), row(col1)),
            pl.BlockSpec((tm, k1), row(col2)),
            pl.BlockSpec(w_out.shape, const),
            pl.BlockSpec((tm, d), row(0)),
            pl.BlockSpec((1, d), const),
            pl.BlockSpec(w_router.shape, const),
        ],
        out_specs=[
            pl.BlockSpec((tm, d), row(0)),
            pl.BlockSpec((tm, d + LANES), row(0)),
            pl.BlockSpec((tm, n_exp), row(0)),
        ],
        out_shape=[
            jax.ShapeDtypeStruct((nt, d), F32),
            jax.ShapeDtypeStruct((nt, d + LANES), F32),
            jax.ShapeDtypeStruct((nt, n_exp), F32),
        ],
        compiler_params=_params("parallel"),
        name="out_proj_router",
    )(a1, a2, w_out, h, ffn_g, w_router)


def _odd_in_kernel(x_ref, g_ref, w_ref, qg_ref, kvg_ref, cos_ref, sin_ref, cq_ref, ckv_ref, kr_ref):
    xn = _rms(x_ref[...], g_ref[...]).astype(BF16)
    p = jnp.dot(xn, w_ref[...], preferred_element_type=F32)
    qr, kvr = cq_ref.shape[1], ckv_ref.shape[1]
    cq_ref[...] = _rms(p[:, :qr], qg_ref[...]).astype(cq_ref.dtype)
    ckv_ref[...] = _rms(p[:, qr: qr + kvr], kvg_ref[...]).astype(ckv_ref.dtype)
    kr_ref[...] = _rope128(p[:, qr + kvr:], cos_ref[...], sin_ref[...]).astype(kr_ref.dtype)


def _odd_in(h, g, w_in_pad, q_g, kv_g, cos128, sin128, seq):
    nt, d = h.shape
    qr, kvr = q_g.shape[1], kv_g.shape[1]
    tm = _tile(seq, 768, 16)
    tiles_per_seq = seq // tm
    row = lambda i: (i, 0)
    const = lambda i: (0, 0)
    pos = lambda i: (i % tiles_per_seq, 0)
    return pl.pallas_call(
        _odd_in_kernel,
        grid=(nt // tm,),
        in_specs=[
            pl.BlockSpec((tm, d), row),
            pl.BlockSpec((1, d), const),
            pl.BlockSpec(w_in_pad.shape, const),
            pl.BlockSpec((1, qr), const),
            pl.BlockSpec((1, kvr), const),
            pl.BlockSpec((tm, LANES), pos),
            pl.BlockSpec((tm, LANES), pos),
        ],
        out_specs=[
            pl.BlockSpec((tm, qr), row),
            pl.BlockSpec((tm, kvr), row),
            pl.BlockSpec((tm, LANES), row),
        ],
        out_shape=[
            jax.ShapeDtypeStruct((nt, qr), BF16),
            jax.ShapeDtypeStruct((nt, kvr), BF16),
            jax.ShapeDtypeStruct((nt, LANES), BF16),
        ],
        compiler_params=_params("parallel"),
        name="odd_in",
    )(h, g, w_in_pad, q_g, kv_g, cos128, sin128)


def _up_proj_kernel(a_ref, w_ref, cos_ref, sin_ref, o_ref, *, rope, scale):
    a = a_ref[...]
    cos, sin = cos_ref[...], sin_ref[...]
    for c in range(o_ref.shape[1] // MLA_QK_PAD):
        lo = c * MLA_QK_PAD
        acc = jnp.dot(a, w_ref[:, lo: lo + MLA_QK_PAD], preferred_element_type=F32)
        if rope:
            o_ref[:, lo: lo + LANES] = (acc[:, :LANES] * scale).astype(o_ref.dtype)
            o_ref[:, lo + LANES: lo + MLA_QK_PAD] = (_rope128(acc[:, LANES:], cos, sin) * scale).astype(o_ref.dtype)
        else:
            o_ref[:, lo: lo + MLA_QK_PAD] = acc.astype(o_ref.dtype)


def _up_proj(a, w, cos128, sin128, seq, rope, scale):
    nt, k = a.shape
    n = w.shape[1]
    tm = _tile(seq, 768, 16)
    tiles_per_seq = seq // tm
    pos = lambda i: (i % tiles_per_seq, 0)
    return pl.pallas_call(
        functools.partial(_up_proj_kernel, rope=rope, scale=scale),
        grid=(nt // tm,),
        in_specs=[
            pl.BlockSpec((tm, k), lambda i: (i, 0)),
            pl.BlockSpec((k, n), lambda i: (0, 0)),
            pl.BlockSpec((tm, LANES), pos),
            pl.BlockSpec((tm, LANES), pos),
        ],
        out_specs=pl.BlockSpec((tm, n), lambda i: (i, 0)),
        out_shape=jax.ShapeDtypeStruct((nt, n), BF16),
        compiler_params=_params("parallel"),
        name="q_up_proj" if rope else "kv_up_proj",
    )(a, w, cos128, sin128)


def _mla_attn_kernel(q_ref, kn_ref, kr_ref, v_ref, o_ref, kcat, vpad, sa, sb, pa, pb, la, lb, *, seq, tq):
    n_heads, lp = kcat.shape[0], kcat.shape[1]
    for h in range(n_heads):
        kcat[h, 0:seq, 0:LANES] = kn_ref[0, :, h * LANES:(h + 1) * LANES]
        kcat[h, 0:seq, LANES:] = kr_ref[0]
        kcat[h, seq:lp, :] = jnp.zeros((lp - seq, MLA_QK_PAD), BF16)
        vpad[h, 0:seq, :] = v_ref[0, :, h * LANES:(h + 1) * LANES]
        vpad[h, seq:lp, :] = jnp.zeros((lp - seq, LANES), BF16)
    tail_bias = _tail_bias(seq, lp)
    per_head = seq // tq
    s_bufs, p_bufs, l_bufs = (sa, sb), (pa, pb), (la, lb)

    def scores(t, par):
        h, r = divmod(t, per_head)
        _store_scores(q_ref[0, r * tq:(r + 1) * tq, h * MLA_QK_PAD:(h + 1) * MLA_QK_PAD], kcat.at[h], tail_bias,
                      s_bufs[par])

    def softmax(par):
        for rows in _row_blocks(tq):
            p, l = _probs(s_bufs[par][rows, :])
            p_bufs[par][rows, :] = p.astype(BF16)
            l_bufs[par][rows, :] = l

    def pv(t, par, _):
        h, r = divmod(t, per_head)
        o = jnp.dot(p_bufs[par][...], vpad[h], preferred_element_type=F32) / l_bufs[par][...]
        o_ref[0, r * tq:(r + 1) * tq, h * LANES:(h + 1) * LANES] = o.astype(o_ref.dtype)

    _attention_pipeline(n_heads * per_head, scores, softmax, pv)


def _mla_attn(q3, kv3, kr3, heads):
    nb, seq, _ = q3.shape
    lp = pl.cdiv(seq, LANES) * LANES
    assert lp > seq
    tq = _tile(seq, ATTN_Q_TILE_MAX, 16)
    hs = MLA_HEADS_PER_STEP
    assert heads % hs == 0
    return pl.pallas_call(
        functools.partial(_mla_attn_kernel, seq=seq, tq=tq),
        grid=(nb, heads // hs),
        in_specs=[
            pl.BlockSpec((1, seq, hs * MLA_QK_PAD), lambda b, h: (b, 0, h)),
            pl.BlockSpec((1, seq, hs * LANES), lambda b, h: (b, 0, h)),
            pl.BlockSpec((1, seq, LANES), lambda b, h: (b, 0, 0)),
            pl.BlockSpec((1, seq, hs * LANES), lambda b, h: (b, 0, heads // hs + h)),
        ],
        out_specs=pl.BlockSpec((1, seq, hs * LANES), lambda b, h: (b, 0, h)),
        out_shape=jax.ShapeDtypeStruct((nb, seq, heads * LANES), BF16),
        scratch_shapes=[pltpu.VMEM((hs, lp, MLA_QK_PAD), BF16), pltpu.VMEM((hs, lp, LANES), BF16)]
        + [pltpu.VMEM((tq, lp), F32)] * 2 + [pltpu.VMEM((tq, lp), BF16)] * 2 + [pltpu.VMEM((tq, 1), F32)] * 2,
        compiler_params=_params("parallel", "parallel"),
        name="mla_attn",
    )(q3, kv3, kr3, kv3)


def _moe_ffn_kernel(idx_ref, tn_hbm, wg_ref, wu_ref, wd_ref, o_ref, xs_buf, sem, *, tm):
    step = pl.program_id(0) * pl.num_programs(1) + pl.program_id(1)
    n_steps = pl.num_programs(0) * pl.num_programs(1)
    slot = step % 2

    def gather(tile, into):
        for r in range(tm):
            tok = idx_ref[tile * tm + r]
            pltpu.make_async_copy(tn_hbm.at[pl.ds(tok, 1), :], xs_buf.at[into, pl.ds(r, 1), :], sem.at[into]).start()

    def wait(which):
        pltpu.make_async_copy(tn_hbm.at[pl.ds(0, tm), :], xs_buf.at[which], sem.at[which]).wait()

    @pl.when(step == 0)
    def _():
        gather(0, 0)

    wait(slot)
    d = o_ref.shape[1]
    xs = xs_buf[slot, :, :d].astype(BF16)
    aff = xs_buf[slot, :, d:]
    lane = lax.broadcasted_iota(jnp.int32, aff.shape, 1)
    gate = jnp.sum(jnp.where(lane == pl.program_id(0), aff, 0.0), axis=-1, keepdims=True)
    hg = jnp.dot(xs, wg_ref[...], preferred_element_type=F32)
    hu = jnp.dot(xs, wu_ref[...], preferred_element_type=F32)
    hid = (hg * jax.nn.sigmoid(hg) * hu).astype(BF16)
    gather(step + 1, 1 - slot)
    o_ref[...] = (jnp.dot(hid, wd_ref[...], preferred_element_type=F32) * gate).astype(o_ref.dtype)

    @pl.when(step == n_steps - 1)
    def _():
        wait(1 - slot)


def _moe_ffn(idx_flat, tn, w_gate, w_up, w_down, layer, slots):
    _, n_exp, d, ff = w_gate.shape
    tm = _tile(slots, FFN_ROWS_MAX, 16)
    tiles = slots // tm
    grid_spec = pltpu.PrefetchScalarGridSpec(
        num_scalar_prefetch=1,
        grid=(n_exp, tiles),
        in_specs=[
            pl.BlockSpec(memory_space=pl.ANY),
            pl.BlockSpec((None, None, d, ff), lambda e, m, idx: (layer, e, 0, 0)),
            pl.BlockSpec((None, None, d, ff), lambda e, m, idx: (layer, e, 0, 0)),
            pl.BlockSpec((None, None, ff, d), lambda e, m, idx: (layer, e, 0, 0)),
        ],
        out_specs=pl.BlockSpec((tm, d), lambda e, m, idx: (e * tiles + m, 0)),
        scratch_shapes=[pltpu.VMEM((2, tm, tn.shape[1]), F32), pltpu.SemaphoreType.DMA((2,))],
    )
    idx_padded = jnp.concatenate([idx_flat, idx_flat[:tm]])
    return pl.pallas_call(
        functools.partial(_moe_ffn_kernel, tm=tm),
        grid_spec=grid_spec,
        out_shape=jax.ShapeDtypeStruct((n_exp * slots, d), BF16),
        compiler_params=_params("arbitrary", "arbitrary"),
        name="moe_ffn",
    )(idx_padded, tn, w_gate, w_up, w_down)


def _combine_kernel(src_ref, pieces_ref, dst_ref, chunks_ref, ye_hbm, h_ref, tgt_ref, o_ref, stage, tgt_b, sem,
                    *, static_chunks):
    i = pl.program_id(0)
    n_tiles = pl.num_programs(0)
    n_exp = tgt_ref.shape[1]
    slot = i % 2

    @pl.when(i == 0)
    def _():
        stage[...] = jnp.zeros(stage.shape, stage.dtype)

    def piece(src_row, into, dst_row):
        return pltpu.make_async_copy(ye_hbm.at[pl.ds(src_row, COMBINE_PIECE), :],
                                     stage.at[into, pl.ds(dst_row, COMBINE_PIECE), :], sem.at[into])

    def fetch(tile, into, live):
        n_pieces = 0
        for e in range(n_exp):
            src, dst = src_ref[tile * n_exp + e], dst_ref[tile * n_exp + e]
            n = pieces_ref[tile * n_exp + e] * live

            def issue(p, carry, src=src, dst=dst):
                piece(pl.multiple_of(src + p * COMBINE_PIECE, COMBINE_PIECE), into,
                      pl.multiple_of(dst + p * COMBINE_PIECE, COMBINE_PIECE)).start()
                return carry

            lax.fori_loop(0, n, issue, 0)
            n_pieces = n_pieces + n
        return n_pieces

    @pl.when(i == 0)
    def _():
        fetch(0, 0, 1)

    n_mine = 0
    for e in range(n_exp):
        n_mine = n_mine + pieces_ref[i * n_exp + e]

    def wait(p, carry):
        piece(0, slot, 0).wait()
        return carry

    lax.fori_loop(0, n_mine, wait, 0)
    nxt = jnp.minimum(i + 1, n_tiles - 1)
    fetch(nxt, 1 - slot, jnp.where(i + 1 < n_tiles, 1, 0))
    t_tile = tgt_ref.shape[0]
    for e in range(n_exp):
        tgt_b[e] = jnp.broadcast_to(tgt_ref[:, e:e + 1], (t_tile, LANES))
    lane = lax.broadcasted_iota(jnp.int32, (1, LANES), 1)

    def onehot(base, n_cols):
        parts = []
        for part in range(n_cols // LANES):
            want = lane + (base + part * LANES)
            hit = jnp.zeros((t_tile, LANES), F32)
            for e in range(n_exp):
                hit = jnp.where(tgt_b[e] == want, 1.0, hit)
            parts.append(hit.astype(BF16))
        return jnp.concatenate(parts, axis=1)

    k0 = static_chunks * COMBINE_CHUNK
    o_ref[...] = h_ref[...] + jnp.dot(onehot(0, k0), stage[slot, 0:k0, :], preferred_element_type=F32)

    def chunk(c, carry):
        base = pl.multiple_of(c * COMBINE_CHUNK, COMBINE_CHUNK)
        o_ref[...] += jnp.dot(onehot(base, COMBINE_CHUNK), stage[slot, pl.ds(base, COMBINE_CHUNK), :],
                              preferred_element_type=F32)
        return carry

    lax.fori_loop(static_chunks, chunks_ref[i], chunk, 0)


def _combine(ye, h, sel_t, row_t, t_tile):
    nt, d = h.shape
    n_exp = sel_t.shape[1]
    assert ye.shape[0] % COMBINE_PIECE == 0
    first = row_t[::t_tile]
    last = (row_t + sel_t)[t_tile - 1::t_tile]
    src = first // COMBINE_PIECE * COMBINE_PIECE
    pieces = (last - src + COMBINE_PIECE - 1) // COMBINE_PIECE
    dst = (jnp.cumsum(pieces, axis=1) - pieces) * COMBINE_PIECE
    chunks = (jnp.sum(pieces, axis=1) * COMBINE_PIECE + COMBINE_CHUNK - 1) // COMBINE_CHUNK
    tgt = jnp.where(sel_t > 0, row_t - jnp.repeat(src - dst, t_tile, axis=0), -1)
    stage_rows = pl.cdiv(n_exp * (t_tile + 2 * (COMBINE_PIECE - 1)), COMBINE_CHUNK) * COMBINE_CHUNK
    usual_rows = n_exp * (N_EXPERT_CAPACITY_FACTOR * t_tile // n_exp + COMBINE_PIECE - 1)
    static_chunks = min(pl.cdiv(usual_rows, COMBINE_CHUNK), stage_rows // COMBINE_CHUNK)
    grid_spec = pltpu.PrefetchScalarGridSpec(
        num_scalar_prefetch=4,
        grid=(nt // t_tile,),
        in_specs=[
            pl.BlockSpec(memory_space=pl.ANY),
            pl.BlockSpec((t_tile, d), lambda i, *_: (i, 0)),
            pl.BlockSpec((t_tile, n_exp), lambda i, *_: (i, 0)),
        ],
        out_specs=pl.BlockSpec((t_tile, d), lambda i, *_: (i, 0)),
        scratch_shapes=[pltpu.VMEM((2, stage_rows, d), ye.dtype), pltpu.VMEM((n_exp, t_tile, LANES), jnp.int32),
                        pltpu.SemaphoreType.DMA((2,))],
    )
    flat = lambda x: x.reshape(-1).astype(jnp.int32)
    return pl.pallas_call(
        functools.partial(_combine_kernel, static_chunks=static_chunks),
        grid_spec=grid_spec,
        out_shape=jax.ShapeDtypeStruct((nt, d), F32),
        compiler_params=_params("arbitrary"),
        name="moe_combine",
    )(flat(src), flat(pieces), flat(dst), flat(chunks), ye, h, tgt.astype(jnp.int32))


def _threshold_kernel(x_ref, tau_ref, *, cap):
    n_exp = x_ref.shape[0]
    cap_f = float(cap)

    def bisect(_, bounds):
        new = []
        for e in range(n_exp):
            lo, hi = bounds[e]
            mid = lo + ((hi - lo) >> 1)
            above = jnp.where(pltpu.bitcast(x_ref[e], jnp.int32) >= mid, 1.0, 0.0)
            ok = jnp.sum(above, axis=(0, 1), keepdims=True) >= cap_f
            new.append((jnp.where(ok, mid, lo), jnp.where(ok, hi, mid)))
        return tuple(new)

    start = (jnp.zeros((1, 1), jnp.int32), jnp.full((1, 1), F32_INF_BITS, jnp.int32))
    bounds = lax.fori_loop(0, 31, bisect, tuple(start for _ in range(n_exp)))
    for e in range(n_exp):
        tau_ref[e] = jnp.broadcast_to(bounds[e][0], tau_ref.shape[1:])


def _route_kernel(x_ref, tau_ref, sel_ref, pexc_ref, idx_ref, *, cap, slot_block):
    x = x_ref[0]
    n_rows = x.shape[0]
    bits = pltpu.bitcast(x, jnp.int32)
    cap_f = float(cap)
    tau = tau_ref[0, 0:1, 0:1]

    upper = jnp.where(lax.broadcasted_iota(jnp.int32, (LANES, LANES), 0)
                      <= lax.broadcasted_iota(jnp.int32, (LANES, LANES), 1), 1.0, 0.0).astype(BF16)
    lower = jnp.where(lax.broadcasted_iota(jnp.int32, (n_rows, n_rows), 1)
                      < lax.broadcasted_iota(jnp.int32, (n_rows, n_rows), 0), 1.0, 0.0).astype(BF16)

    def prefix(m):
        within = jnp.dot(m.astype(BF16), upper, preferred_element_type=F32)
        row_tot = jnp.broadcast_to(within[:, LANES - 1:], within.shape)
        before = jnp.dot(lower, row_tot.astype(BF16), preferred_element_type=F32)
        return within, before, row_tot

    gt = jnp.where(bits > tau, 1.0, 0.0)
    eq = jnp.where(bits == tau, 1.0, 0.0)
    need = cap_f - jnp.sum(gt, axis=(0, 1), keepdims=True)
    tie_within, tie_before, _ = prefix(eq)
    tie_rank = tie_within + tie_before - eq
    sel = gt + eq * jnp.where(tie_rank < need, 1.0, 0.0)
    within, before, row_tot = prefix(sel)
    sel_ref[0] = sel.astype(jnp.int32)
    pexc_ref[0] = (within + before - sel).astype(jnp.int32)

    row_end = jnp.transpose(before + row_tot)[0:1, :]
    row_start = jnp.transpose(before)[0:1, :]
    within_b = within.astype(BF16)
    row_id = lax.broadcasted_iota(jnp.int32, (1, n_rows), 1).astype(F32)
    for s in range(idx_ref.shape[1] // slot_block):
        j = (lax.broadcasted_iota(jnp.int32, (slot_block, 1), 0) + s * slot_block).astype(F32)
        row = jnp.sum(jnp.where(row_end <= j, 1.0, 0.0), axis=-1, keepdims=True)
        onehot = jnp.where(row_id == row, 1.0, 0.0)
        cum = jnp.dot(onehot.astype(BF16), within_b, preferred_element_type=F32)
        local = j - jnp.sum(onehot * row_start, axis=-1, keepdims=True)
        col = jnp.sum(jnp.where(cum <= local, 1.0, 0.0), axis=-1, keepdims=True)
        tok = (row * LANES + col).astype(jnp.int32)
        idx_ref[0, s * slot_block:(s + 1) * slot_block, :] = jnp.where(j < cap_f, tok, 0)


def _route(aff_rows, cap):
    n_exp, n_rows, _ = aff_rows.shape
    cap_pad = pl.cdiv(cap, ROUTE_SLOT_BLOCK) * ROUTE_SLOT_BLOCK
    blk = pl.BlockSpec((1, n_rows, LANES), lambda e: (e, 0, 0))
    tau = pl.pallas_call(
        functools.partial(_threshold_kernel, cap=cap),
        out_shape=jax.ShapeDtypeStruct((n_exp, 8, LANES), jnp.int32),
        compiler_params=pltpu.CompilerParams(vmem_limit_bytes=VMEM_LIMIT_BYTES),
        name="route_threshold",
    )(aff_rows)
    return pl.pallas_call(
        functools.partial(_route_kernel, cap=cap, slot_block=ROUTE_SLOT_BLOCK),
        grid=(n_exp,),
        in_specs=[blk, pl.BlockSpec((1, 8, LANES), lambda e: (e, 0, 0))],
        out_specs=[blk, blk, pl.BlockSpec((1, cap_pad, 1), lambda e: (e, 0, 0))],
        out_shape=[
            jax.ShapeDtypeStruct(aff_rows.shape, jnp.int32),
            jax.ShapeDtypeStruct(aff_rows.shape, jnp.int32),
            jax.ShapeDtypeStruct((n_exp, cap_pad, 1), jnp.int32),
        ],
        compiler_params=_params("parallel"),
        name="route",
    )(aff_rows, tau)


def _moe(h, tn, aff, group_sizes, w_gate, w_up, w_down, layer):
    n_exp = aff.shape[1]
    caps = [N_EXPERT_CAPACITY_FACTOR * n // n_exp for n in group_sizes]
    slots = pl.cdiv(sum(caps), FFN_SLOT_ALIGN) * FFN_SLOT_ALIGN
    idx_parts, sel_parts, row_parts = [], [], []
    tok0, slot0 = 0, 0
    for n, cap in zip(group_sizes, caps):
        n_rows = pl.cdiv(pl.cdiv(n, LANES), LANES) * LANES
        rows = jnp.pad(aff[tok0: tok0 + n].T, ((0, 0), (0, n_rows * LANES - n)), constant_values=-1.0)
        sel, pexc, idx = _route(rows.reshape(n_exp, n_rows, LANES), cap)
        idx_parts.append(idx[:, :cap, 0] + tok0)
        sel_parts.append(sel.reshape(n_exp, -1)[:, :n].T)
        expert_row0 = jnp.arange(n_exp, dtype=jnp.int32)[None, :] * slots + slot0
        row_parts.append(pexc.reshape(n_exp, -1)[:, :n].T + expert_row0)
        tok0 += n
        slot0 += cap
    idx_parts.append(jnp.zeros((n_exp, slots - sum(caps)), jnp.int32))
    idx_all = jnp.concatenate(idx_parts, axis=1)
    sel_t = jnp.concatenate(sel_parts, axis=0)
    row_t = jnp.concatenate(row_parts, axis=0)
    ye = _moe_ffn(idx_all.reshape(-1), tn, w_gate, w_up, w_down, layer, slots)
    t_tile = _tile(math.gcd(*group_sizes), COMBINE_TOKENS_MAX, 8)
    return _combine(ye, h, sel_t, row_t, t_tile)


def _final_norm_kernel(x_ref, g_ref, o_ref):
    o_ref[...] = _rms(x_ref[...], g_ref[...])


def _final_norm(h, g, first_seq, n_seqs, seq):
    _, d = h.shape
    s_out = seq - N_META
    tm = _tile(s_out, 512, 8)
    per_seq = s_out // tm
    out = pl.pallas_call(
        _final_norm_kernel,
        grid=(n_seqs, per_seq),
        in_specs=[
            pl.BlockSpec((pl.Element(tm), pl.Element(d)),
                         lambda b, j: (pl.multiple_of((first_seq + b) * seq + N_META + j * tm, 8), 0)),
            pl.BlockSpec((1, d), lambda b, j: (0, 0)),
        ],
        out_specs=pl.BlockSpec((tm, d), lambda b, j: (b * per_seq + j, 0)),
        out_shape=jax.ShapeDtypeStruct((n_seqs * s_out, d), F32),
        compiler_params=_params("parallel", "parallel"),
        name="final_norm",
    )(h, g)
    return out.reshape(n_seqs, s_out, d)


def _lambda_init(layer):
    return 0.8 - 0.6 * math.exp(-0.3 * layer)


def kernel(x_prompt, x_sample, meta_tokens, mix_norm_g, ffn_norm_g, final_norm_g, ev_w_in, ev_w_out, ev_lambda_q1, ev_lambda_k1, ev_lambda_q2, ev_lambda_k2, ev_subln_g, ev_conv_w, od_w_in, od_q_norm_g, od_w_q_b, od_kv_norm_g, od_w_kv_b, od_w_out, moe_w_router, moe_w_gate, moe_w_up, moe_w_down):
    d = x_prompt.shape[-1]
    seq = N_META + x_prompt.shape[1]
    assert x_sample.shape[1] == x_prompt.shape[1]
    nb_p, nb_s = x_prompt.shape[0], x_sample.shape[0]
    nb = nb_p + nb_s
    nt = nb * seq
    group_sizes = (nb_p * seq, nb_s * seq)
    depth = mix_norm_g.shape[0]

    x = jnp.concatenate([x_prompt, x_sample], axis=0)
    meta = jnp.broadcast_to(meta_tokens[None].astype(x.dtype), (nb, N_META, d))
    h = jnp.concatenate([meta, x], axis=1).reshape(nt, d)
    cos128, sin128 = _rope_tables(seq)
    w_gate, w_up, w_down = moe_w_gate.astype(BF16), moe_w_up.astype(BF16), moe_w_down.astype(BF16)

    for layer in range(depth):
        i = layer // 2
        mix_g = mix_norm_g[layer][None]
        ffn_g = ffn_norm_g[layer][None]
        w_router = moe_w_router[layer].astype(BF16)
        if layer % 2 == 0:
            width = d // 2
            heads = width // LANES
            w_in = ev_w_in[i].astype(BF16)
            qk3 = _even_proj(h, mix_g, w_in, cos128, sin128, seq, True).reshape(nb, seq, 2 * width)
            rest3 = _even_proj(h, mix_g, w_in, cos128, sin128, seq, False).reshape(nb, seq, 4 * width)
            attn = _diff_attn(qk3, rest3, ev_lambda_q1[i][None], ev_lambda_k1[i][None], ev_lambda_q2[i][None],
                              ev_lambda_k2[i][None], ev_subln_g[i][None], _lambda_init(layer), heads)
            conv = _gated_conv(rest3, ev_conv_w[i], width)
            h, tn, aff = _out_proj_router(attn.reshape(nt, width), conv.reshape(nt, width), 0, 0,
                                          ev_w_out[i].astype(BF16), h, ffn_g, w_router)
        else:
            q_rank, kv_rank = od_q_norm_g.shape[1], od_kv_norm_g.shape[1]
            heads = od_w_q_b.shape[2] // (MLA_NOPE + MLA_ROPE)
            w_in_pad = jnp.pad(od_w_in[i], ((0, 0), (0, LANES - MLA_ROPE))).astype(BF16)
            w_q = od_w_q_b[i].reshape(q_rank, heads, MLA_NOPE + MLA_ROPE)
            w_q = jnp.pad(w_q, ((0, 0), (0, 0), (0, MLA_QK_PAD - MLA_NOPE - MLA_ROPE)))
            w_q = w_q.reshape(q_rank, heads * MLA_QK_PAD).astype(BF16)
            w_kv = od_w_kv_b[i].reshape(kv_rank, heads, MLA_NOPE + MLA_V)
            w_kv = jnp.concatenate([w_kv[:, :, :MLA_NOPE].reshape(kv_rank, heads * MLA_NOPE),
                                    w_kv[:, :, MLA_NOPE:].reshape(kv_rank, heads * MLA_V)], axis=1).astype(BF16)
            cq, ckv, kr = _odd_in(h, mix_g, w_in_pad, od_q_norm_g[i][None], od_kv_norm_g[i][None], cos128, sin128, seq)
            q = _up_proj(cq, w_q, cos128, sin128, seq, True, (MLA_NOPE + MLA_ROPE) ** -0.5 * LOG2E)
            kv = _up_proj(ckv, w_kv, cos128, sin128, seq, False, 1.0)
            o = _mla_attn(q.reshape(nb, seq, heads * MLA_QK_PAD), kv.reshape(nb, seq, 2 * heads * LANES),
                          kr.reshape(nb, seq, LANES), heads)
            o = o.reshape(nt, heads * MLA_V)
            h, tn, aff = _out_proj_router(o, o, 0, 1, od_w_out[i].astype(BF16), h, ffn_g, w_router)
        h = _moe(h, tn, aff, group_sizes, w_gate, w_up, w_down, layer)

    g = final_norm_g[None]
    return (_final_norm(h, g, 0, nb_p, seq), _final_norm(h, g, nb_p, nb_s, seq))
```

```python
import functools
import math

import jax
import jax.numpy as jnp
from jax import lax
from jax.experimental import pallas as pl
from jax.experimental.pallas import tpu as pltpu

N_META = 16
EPS = 1e-6
ROPE_THETA = 10000.0
HEAD_DIM = 64
LANES = 128
MXU_COLS = 256
MLA_NOPE = 128
MLA_ROPE = 64
MLA_V = 128
MLA_QK_PAD = 256
ATTN_Q_TILE_MAX = 768
ATTN_ROW_BLOCK = 16
MLA_HEADS_PER_STEP = 2
DIFF_HEADS_PER_STEP = 1
N_EXPERT_CAPACITY_FACTOR = 2
VMEM_LIMIT_BYTES = 56 * 1024 * 1024
NEG_BIG = -1e30
F32_INF_BITS = 0x7F800000
LOG2E = math.log2(math.e)
ROUTE_SLOT_BLOCK = 512
ROUTE_REFINE_STEPS = 24
COMBINE_TOKENS_MAX = 192
COMBINE_PIECE = 16
FFN_ROWS_MAX = 448
FFN_SLOT_ALIGN = 64
WEIGHT_CAST_ROWS = 128
COMBINE_CHUNK = 256

F32 = jnp.float32
BF16 = jnp.bfloat16


def _tile(n, cap, mult):
    best = None
    for t in range(mult, min(n, cap) + 1, mult):
        if n % t == 0:
            best = t
    assert best is not None, (n, cap, mult)
    return best


def _params(*sem):
    return pltpu.CompilerParams(dimension_semantics=sem, vmem_limit_bytes=VMEM_LIMIT_BYTES)


def _rms(x, g):
    return x * lax.rsqrt(jnp.mean(x * x, axis=-1, keepdims=True) + EPS) * g


def _rope128(x, cos, sin_signed):
    lane = lax.broadcasted_iota(jnp.int32, x.shape, 1)
    first_half = (lane % HEAD_DIM) < (HEAD_DIM // 2)
    rot = jnp.where(first_half, pltpu.roll(x, LANES - HEAD_DIM // 2, 1), pltpu.roll(x, HEAD_DIM // 2, 1))
    return x * cos + rot * sin_signed


def _rope_tables(length):
    inv = 1.0 / (ROPE_THETA ** (jnp.arange(0, HEAD_DIM, 2, dtype=F32) / HEAD_DIM))
    ang = jnp.arange(length, dtype=F32)[:, None] * inv[None, :]
    cos, sin = jnp.cos(ang), jnp.sin(ang)
    cos128 = jnp.tile(cos, (1, LANES // (HEAD_DIM // 2)))
    sin128 = jnp.tile(jnp.concatenate([-sin, sin], axis=1), (1, LANES // HEAD_DIM))
    return cos128, sin128


def _even_proj_kernel(x_ref, g_ref, w_ref, cos_ref, sin_ref, o_ref, xn_ref, *, rope, q_scale):
    j = pl.program_id(1)

    @pl.when(j == 0)
    def _():
        xn_ref[...] = _rms(x_ref[...], g_ref[...]).astype(BF16)

    xn = xn_ref[...]
    if rope:
        scale = jnp.where(j == 0, q_scale, 1.0).astype(F32)
        cos, sin = cos_ref[...], sin_ref[...]
    for c in range(o_ref.shape[1] // MXU_COLS):
        lo = c * MXU_COLS
        acc = jnp.dot(xn, w_ref[:, lo: lo + MXU_COLS], preferred_element_type=F32)
        if rope:
            for half in range(MXU_COLS // LANES):
                sl = slice(half * LANES, (half + 1) * LANES)
                o_ref[:, lo + half * LANES: lo + (half + 1) * LANES] = (
                    _rope128(acc[:, sl], cos, sin) * scale).astype(o_ref.dtype)
        else:
            o_ref[:, lo: lo + MXU_COLS] = acc.astype(o_ref.dtype)


def _even_proj(h, g, w_in, cos128, sin128, seq, rope):
    nt, d = h.shape
    tn = w_in.shape[1] // 6
    col0, n_cols = (0, 2) if rope else (2, 4)
    tm = _tile(seq, 768, 16)
    tiles_per_seq = seq // tm
    return pl.pallas_call(
        functools.partial(_even_proj_kernel, rope=rope, q_scale=HEAD_DIM ** -0.5 * LOG2E),
        grid=(nt // tm, n_cols),
        in_specs=[
            pl.BlockSpec((tm, d), lambda i, j: (i, 0)),
            pl.BlockSpec((1, d), lambda i, j: (0, 0)),
            pl.BlockSpec((d, tn), lambda i, j: (0, col0 + j)),
            pl.BlockSpec((tm, LANES), lambda i, j: (i % tiles_per_seq, 0)),
            pl.BlockSpec((tm, LANES), lambda i, j: (i % tiles_per_seq, 0)),
        ],
        out_specs=pl.BlockSpec((tm, tn), lambda i, j: (i, j)),
        out_shape=jax.ShapeDtypeStruct((nt, n_cols * tn), BF16),
        scratch_shapes=[pltpu.VMEM((tm, d), BF16)],
        compiler_params=_params("parallel", "arbitrary"),
        name="even_proj_qk" if rope else "even_proj_rest",
    )(h, g, w_in, cos128, sin128)


def _store_scores(q, k_ref, tail_bias, s_ref):
    s = _nt_dot(q, k_ref[...])
    n_main = s.shape[1] - LANES
    s_ref[:, :n_main] = s[:, :n_main]
    s_ref[:, n_main:] = s[:, n_main:] + tail_bias


def _probs(s):
    p = jnp.exp2(s - jnp.max(s, axis=-1, keepdims=True))
    return p, jnp.sum(p, axis=-1, keepdims=True)


def _row_blocks(n_rows):
    return [slice(r, r + ATTN_ROW_BLOCK) for r in range(0, n_rows, ATTN_ROW_BLOCK)]


def _tail_bias(seq, lp):
    lane = lax.broadcasted_iota(jnp.int32, (1, LANES), 1)
    return jnp.where(lane < seq - (lp - LANES), 0.0, NEG_BIG).astype(F32)


def _nt_dot(a, b):
    return lax.dot_general(a, b, (((1,), (1,)), ((), ())), preferred_element_type=F32)


def _attention_pipeline(n_tiles, scores, softmax, pv):
    sums = {}
    for step in range(n_tiles + 2):
        if 0 <= step - 2 < n_tiles:
            pv(step - 2, (step - 2) % 2, sums.pop(step - 2))
        if 0 <= step - 1 < n_tiles:
            sums[step - 1] = softmax((step - 1) % 2)
        if step < n_tiles:
            scores(step, step % 2)


def _diff_attn_kernel(lq1_ref, lk1_ref, lq2_ref, lk2_ref, q_ref, k_ref, v_ref, g_ref, o_ref, kpad, vpad,
                      s1a, s1b, s2a, s2b, pa, pb, *, seq, lam_init, tq):
    n_heads, lp = kpad.shape[0], kpad.shape[1]
    for h in range(n_heads):
        cols = slice(h * LANES, (h + 1) * LANES)
        kpad[h, 0:seq, :] = k_ref[0, :, cols]
        kpad[h, seq:lp, :] = jnp.zeros((lp - seq, LANES), BF16)
        vpad[h, 0:seq, :] = v_ref[0, :, cols]
        vpad[h, seq:lp, :] = jnp.zeros((lp - seq, LANES), BF16)
    lam = (jnp.exp(jnp.sum(lq1_ref[...] * lk1_ref[...], axis=-1, keepdims=True))
           - jnp.exp(jnp.sum(lq2_ref[...] * lk2_ref[...], axis=-1, keepdims=True)) + lam_init)
    tail_bias = _tail_bias(seq, lp)
    gain = g_ref[...] * (1.0 - lam_init)
    per_head = seq // tq
    s_bufs, p_bufs = ((s1a, s2a), (s1b, s2b)), (pa, pb)

    def scores(t, par):
        h, r = divmod(t, per_head)
        q = q_ref[0, r * tq:(r + 1) * tq, h * LANES:(h + 1) * LANES]
        lane = lax.broadcasted_iota(jnp.int32, q.shape, 1)
        zero = jnp.zeros_like(q)
        _store_scores(jnp.where(lane < HEAD_DIM, q, zero), kpad.at[h], tail_bias, s_bufs[par][0])
        _store_scores(jnp.where(lane >= HEAD_DIM, q, zero), kpad.at[h], tail_bias, s_bufs[par][1])

    def softmax(par):
        for rows in _row_blocks(tq):
            p1, l1 = _probs(s_bufs[par][0][rows, :])
            p2, l2 = _probs(s_bufs[par][1][rows, :])
            p_bufs[par][rows, :] = (p1 * (1.0 / l1) - p2 * (lam / l2)).astype(BF16)

    def pv(t, par, _):
        h, r = divmod(t, per_head)
        o = jnp.dot(p_bufs[par][...], vpad[h], preferred_element_type=F32)
        o = o * lax.rsqrt(jnp.mean(o * o, axis=-1, keepdims=True) + EPS) * gain
        o_ref[0, r * tq:(r + 1) * tq, h * LANES:(h + 1) * LANES] = o.astype(o_ref.dtype)

    _attention_pipeline(n_heads * per_head, scores, softmax, pv)


def _diff_attn(qk3, rest3, lq1, lk1, lq2, lk2, subln_g, lam_init, heads):
    nb, seq, _ = qk3.shape
    lp = pl.cdiv(seq, LANES) * LANES
    assert lp > seq
    tq = _tile(seq, ATTN_Q_TILE_MAX, 16)
    vec = lambda: pl.BlockSpec((1, HEAD_DIM), lambda b, h: (0, 0))
    hs = DIFF_HEADS_PER_STEP
    assert heads % hs == 0
    return pl.pallas_call(
        functools.partial(_diff_attn_kernel, seq=seq, lam_init=lam_init, tq=tq),
        grid=(nb, heads // hs),
        in_specs=[
            vec(), vec(), vec(), vec(),
            pl.BlockSpec((1, seq, hs * LANES), lambda b, h: (b, 0, h)),
            pl.BlockSpec((1, seq, hs * LANES), lambda b, h: (b, 0, heads // hs + h)),
            pl.BlockSpec((1, seq, hs * LANES), lambda b, h: (b, 0, h)),
            pl.BlockSpec((1, LANES), lambda b, h: (0, 0)),
        ],
        out_specs=pl.BlockSpec((1, seq, hs * LANES), lambda b, h: (b, 0, h)),
        out_shape=jax.ShapeDtypeStruct((nb, seq, heads * LANES), BF16),
        scratch_shapes=[pltpu.VMEM((hs, lp, LANES), BF16), pltpu.VMEM((hs, lp, LANES), BF16)]
        + [pltpu.VMEM((tq, lp), F32)] * 4 + [pltpu.VMEM((tq, lp), BF16)] * 2,
        compiler_params=_params("parallel", "parallel"),
        name="diff_attn",
    )(lq1, lk1, lq2, lk2, qk3, qk3, rest3, subln_g)


def _conv_kernel(gb_ref, gc_ref, cx_ref, w_ref, o_ref):
    u = gc_ref[0].astype(F32) * cx_ref[0].astype(F32)
    seq = u.shape[0]
    row = lax.broadcasted_iota(jnp.int32, u.shape, 0)
    prev = jnp.where(row == 0, 0.0, pltpu.roll(u, 1, 0))
    nxt = jnp.where(row == seq - 1, 0.0, pltpu.roll(u, seq - 1, 0))
    y = prev * w_ref[0:1, :] + u * w_ref[1:2, :] + nxt * w_ref[2:3, :]
    o_ref[0] = (gb_ref[0].astype(F32) * y).astype(o_ref.dtype)


def _gated_conv(rest3, conv_w, width):
    nb, seq, _ = rest3.shape
    tc = _tile(width, 256, LANES)
    nc = width // tc
    blk = lambda part: pl.BlockSpec((1, seq, tc), lambda b, c: (b, 0, part * nc + c))
    return pl.pallas_call(
        _conv_kernel,
        grid=(nb, nc),
        in_specs=[blk(1), blk(2), blk(3), pl.BlockSpec((3, tc), lambda b, c: (0, c))],
        out_specs=pl.BlockSpec((1, seq, tc), lambda b, c: (b, 0, c)),
        out_shape=jax.ShapeDtypeStruct((nb, seq, width), BF16),
        compiler_params=_params("parallel", "parallel"),
        name="gated_conv",
    )(rest3, rest3, rest3, conv_w)


def _out_proj_router_kernel(a1_ref, a2_ref, w_ref, h_ref, g_ref, wr_ref, hn_ref, tn_ref, aff_ref):
    k1 = a1_ref.shape[1]
    y = jnp.dot(a1_ref[...], w_ref[0:k1, :], preferred_element_type=F32)
    y = y + jnp.dot(a2_ref[...], w_ref[k1:, :], preferred_element_type=F32)
    h = h_ref[...] + y
    hn_ref[...] = h
    t = _rms(h, g_ref[...])
    d = t.shape[1]
    n_exp = aff_ref.shape[1]
    logits = jnp.dot(t.astype(BF16), wr_ref[...], preferred_element_type=F32)
    lane = lax.broadcasted_iota(jnp.int32, logits.shape, 1)
    logits = jnp.where(lane < n_exp, logits, NEG_BIG)
    e = jnp.exp(logits - jnp.max(logits, axis=-1, keepdims=True))
    aff = e / jnp.sum(e, axis=-1, keepdims=True)
    aff_ref[...] = aff[:, :n_exp]
    tn_ref[:, :d] = t
    tn_ref[:, d:] = aff


def _out_proj_router(a1, a2, col1, col2, w_out, h, ffn_g, w_router):
    nt, d = h.shape
    k1 = w_out.shape[0] // 2
    n_exp = w_router.shape[1]
    w_router = jnp.pad(w_router, ((0, 0), (0, LANES - n_exp)))
    tm = _tile(nt, 512, 16)
    row = lambda c: (lambda i: (i, c))
    const = lambda i: (0, 0)
    return pl.pallas_call(
        _out_proj_router_kernel,
        grid=(nt // tm,),
        in_specs=[
            pl.BlockSpec((tm, k1), row(col1)),
            pl.BlockSpec((tm, k1), row(col2)),
            pl.BlockSpec(w_out.shape, const),
            pl.BlockSpec((tm, d), row(0)),
            pl.BlockSpec((1, d), const),
            pl.BlockSpec(w_router.shape, const),
        ],
        out_specs=[
            pl.BlockSpec((tm, d), row(0)),
            pl.BlockSpec((tm, d + LANES), row(0)),
            pl.BlockSpec((tm, n_exp), row(0)),
        ],
        out_shape=[
            jax.ShapeDtypeStruct((nt, d), F32),
            jax.ShapeDtypeStruct((nt, d + LANES), F32),
            jax.ShapeDtypeStruct((nt, n_exp), F32),
        ],
        compiler_params=_params("parallel"),
        name="out_proj_router",
    )(a1, a2, w_out, h, ffn_g, w_router)


def _odd_in_kernel(x_ref, g_ref, w_ref, qg_ref, kvg_ref, cos_ref, sin_ref, cq_ref, ckv_ref, kr_ref):
    xn = _rms(x_ref[...], g_ref[...]).astype(BF16)
    p = jnp.dot(xn, w_ref[...], preferred_element_type=F32)
    qr, kvr = cq_ref.shape[1], ckv_ref.shape[1]
    cq_ref[...] = _rms(p[:, :qr], qg_ref[...]).astype(cq_ref.dtype)
    ckv_ref[...] = _rms(p[:, qr: qr + kvr], kvg_ref[...]).astype(ckv_ref.dtype)
    kr_ref[...] = _rope128(p[:, qr + kvr:], cos_ref[...], sin_ref[...]).astype(kr_ref.dtype)


def _odd_in(h, g, w_in_pad, q_g, kv_g, cos128, sin128, seq):
    nt, d = h.shape
    qr, kvr = q_g.shape[1], kv_g.shape[1]
    tm = _tile(seq, 768, 16)
    tiles_per_seq = seq // tm
    row = lambda i: (i, 0)
    const = lambda i: (0, 0)
    pos = lambda i: (i % tiles_per_seq, 0)
    return pl.pallas_call(
        _odd_in_kernel,
        grid=(nt // tm,),
        in_specs=[
            pl.BlockSpec((tm, d), row),
            pl.BlockSpec((1, d), const),
            pl.BlockSpec(w_in_pad.shape, const),
            pl.BlockSpec((1, qr), const),
            pl.BlockSpec((1, kvr), const),
            pl.BlockSpec((tm, LANES), pos),
            pl.BlockSpec((tm, LANES), pos),
        ],
        out_specs=[
            pl.BlockSpec((tm, qr), row),
            pl.BlockSpec((tm, kvr), row),
            pl.BlockSpec((tm, LANES), row),
        ],
        out_shape=[
            jax.ShapeDtypeStruct((nt, qr), BF16),
            jax.ShapeDtypeStruct((nt, kvr), BF16),
            jax.ShapeDtypeStruct((nt, LANES), BF16),
        ],
        compiler_params=_params("parallel"),
        name="odd_in",
    )(h, g, w_in_pad, q_g, kv_g, cos128, sin128)


def _up_proj_kernel(a_ref, w_ref, cos_ref, sin_ref, o_ref, *, rope, scale):
    a = a_ref[...]
    cos, sin = cos_ref[...], sin_ref[...]
    for c in range(o_ref.shape[1] // MLA_QK_PAD):
        lo = c * MLA_QK_PAD
        acc = jnp.dot(a, w_ref[:, lo: lo + MLA_QK_PAD], preferred_element_type=F32)
        if rope:
            o_ref[:, lo: lo + LANES] = (acc[:, :LANES] * scale).astype(o_ref.dtype)
            o_ref[:, lo + LANES: lo + MLA_QK_PAD] = (_rope128(acc[:, LANES:], cos, sin) * scale).astype(o_ref.dtype)
        else:
            o_ref[:, lo: lo + MLA_QK_PAD] = acc.astype(o_ref.dtype)


def _up_proj(a, w, cos128, sin128, seq, rope, scale):
    nt, k = a.shape
    n = w.shape[1]
    tm = _tile(seq, 768, 16)
    tiles_per_seq = seq // tm
    pos = lambda i: (i % tiles_per_seq, 0)
    return pl.pallas_call(
        functools.partial(_up_proj_kernel, rope=rope, scale=scale),
        grid=(nt // tm,),
        in_specs=[
            pl.BlockSpec((tm, k), lambda i: (i, 0)),
            pl.BlockSpec((k, n), lambda i: (0, 0)),
            pl.BlockSpec((tm, LANES), pos),
            pl.BlockSpec((tm, LANES), pos),
        ],
        out_specs=pl.BlockSpec((tm, n), lambda i: (i, 0)),
        out_shape=jax.ShapeDtypeStruct((nt, n), BF16),
        compiler_params=_params("parallel"),
        name="q_up_proj" if rope else "kv_up_proj",
    )(a, w, cos128, sin128)


def _mla_attn_kernel(q_ref, kn_ref, kr_ref, v_ref, o_ref, kcat, vpad, sa, sb, pa, pb, la, lb, *, seq, tq):
    n_heads, lp = kcat.shape[0], kcat.shape[1]
    for h in range(n_heads):
        kcat[h, 0:seq, 0:LANES] = kn_ref[0, :, h * LANES:(h + 1) * LANES]
        kcat[h, 0:seq, LANES:] = kr_ref[0]
        kcat[h, seq:lp, :] = jnp.zeros((lp - seq, MLA_QK_PAD), BF16)
        vpad[h, 0:seq, :] = v_ref[0, :, h * LANES:(h + 1) * LANES]
        vpad[h, seq:lp, :] = jnp.zeros((lp - seq, LANES), BF16)
    tail_bias = _tail_bias(seq, lp)
    per_head = seq // tq
    s_bufs, p_bufs, l_bufs = (sa, sb), (pa, pb), (la, lb)

    def scores(t, par):
        h, r = divmod(t, per_head)
        _store_scores(q_ref[0, r * tq:(r + 1) * tq, h * MLA_QK_PAD:(h + 1) * MLA_QK_PAD], kcat.at[h], tail_bias,
                      s_bufs[par])

    def softmax(par):
        for rows in _row_blocks(tq):
            p, l = _probs(s_bufs[par][rows, :])
            p_bufs[par][rows, :] = p.astype(BF16)
            l_bufs[par][rows, :] = l

    def pv(t, par, _):
        h, r = divmod(t, per_head)
        o = jnp.dot(p_bufs[par][...], vpad[h], preferred_element_type=F32) / l_bufs[par][...]
        o_ref[0, r * tq:(r + 1) * tq, h * LANES:(h + 1) * LANES] = o.astype(o_ref.dtype)

    _attention_pipeline(n_heads * per_head, scores, softmax, pv)


def _mla_attn(q3, kv3, kr3, heads):
    nb, seq, _ = q3.shape
    lp = pl.cdiv(seq, LANES) * LANES
    assert lp > seq
    tq = _tile(seq, ATTN_Q_TILE_MAX, 16)
    hs = MLA_HEADS_PER_STEP
    assert heads % hs == 0
    return pl.pallas_call(
        functools.partial(_mla_attn_kernel, seq=seq, tq=tq),
        grid=(nb, heads // hs),
        in_specs=[
            pl.BlockSpec((1, seq, hs * MLA_QK_PAD), lambda b, h: (b, 0, h)),
            pl.BlockSpec((1, seq, hs * LANES), lambda b, h: (b, 0, h)),
            pl.BlockSpec((1, seq, LANES), lambda b, h: (b, 0, 0)),
            pl.BlockSpec((1, seq, hs * LANES), lambda b, h: (b, 0, heads // hs + h)),
        ],
        out_specs=pl.BlockSpec((1, seq, hs * LANES), lambda b, h: (b, 0, h)),
        out_shape=jax.ShapeDtypeStruct((nb, seq, heads * LANES), BF16),
        scratch_shapes=[pltpu.VMEM((hs, lp, MLA_QK_PAD), BF16), pltpu.VMEM((hs, lp, LANES), BF16)]
        + [pltpu.VMEM((tq, lp), F32)] * 2 + [pltpu.VMEM((tq, lp), BF16)] * 2 + [pltpu.VMEM((tq, 1), F32)] * 2,
        compiler_params=_params("parallel", "parallel"),
        name="mla_attn",
    )(q3, kv3, kr3, kv3)


def _moe_ffn_kernel(idx_ref, tn_hbm, wg_hbm, wu_hbm, wd_hbm, o_ref, xs_buf, wg32, wu32, wd32, wg_ref, wu_ref, wd_ref,
                    sem, wsem, *, tm, layer):
    expert, n_exp = pl.program_id(0), pl.num_programs(0)
    step = expert * pl.num_programs(1) + pl.program_id(1)
    n_steps = n_exp * pl.num_programs(1)
    slot = step % 2
    staged = ((wg_hbm, wg32, wg_ref), (wu_hbm, wu32, wu_ref), (wd_hbm, wd32, wd_ref))

    def weight_copies(e):
        return [pltpu.make_async_copy(src.at[layer, e], stage, wsem) for src, stage, _ in staged]

    @pl.when(step == 0)
    def _():
        for copy in weight_copies(0):
            copy.start()

    @pl.when(pl.program_id(1) == 0)
    def _():
        for copy in weight_copies(expert):
            copy.wait()
        for _, stage, w_ref in staged:
            def cast_rows(c, carry, stage=stage, w_ref=w_ref):
                rows = pl.ds(pl.multiple_of(c * WEIGHT_CAST_ROWS, WEIGHT_CAST_ROWS), WEIGHT_CAST_ROWS)
                w_ref[rows, :] = stage[rows, :].astype(BF16)
                return carry

            lax.fori_loop(0, stage.shape[0] // WEIGHT_CAST_ROWS, cast_rows, 0)

        @pl.when(expert + 1 < n_exp)
        def _():
            for copy in weight_copies(expert + 1):
                copy.start()


    def gather(tile, into):
        for r in range(tm):
            tok = idx_ref[tile * tm + r]
            pltpu.make_async_copy(tn_hbm.at[pl.ds(tok, 1), :], xs_buf.at[into, pl.ds(r, 1), :], sem.at[into]).start()

    def wait(which):
        pltpu.make_async_copy(tn_hbm.at[pl.ds(0, tm), :], xs_buf.at[which], sem.at[which]).wait()

    @pl.when(step == 0)
    def _():
        gather(0, 0)

    wait(slot)
    d = o_ref.shape[1]
    xs = xs_buf[slot, :, :d].astype(BF16)
    aff = xs_buf[slot, :, d:]
    lane = lax.broadcasted_iota(jnp.int32, aff.shape, 1)
    gate = jnp.sum(jnp.where(lane == pl.program_id(0), aff, 0.0), axis=-1, keepdims=True)
    hg = jnp.dot(xs, wg_ref[...], preferred_element_type=F32)
    hu = jnp.dot(xs, wu_ref[...], preferred_element_type=F32)
    hid = (hg * jax.nn.sigmoid(hg) * hu).astype(BF16)
    gather(step + 1, 1 - slot)
    o_ref[...] = (jnp.dot(hid, wd_ref[...], preferred_element_type=F32) * gate).astype(o_ref.dtype)

    @pl.when(step == n_steps - 1)
    def _():
        wait(1 - slot)


def _moe_ffn(idx_flat, tn, w_gate, w_up, w_down, layer, slots):
    _, n_exp, d, ff = w_gate.shape
    tm = _tile(slots, FFN_ROWS_MAX, 16)
    tiles = slots // tm
    grid_spec = pltpu.PrefetchScalarGridSpec(
        num_scalar_prefetch=1,
        grid=(n_exp, tiles),
        in_specs=[pl.BlockSpec(memory_space=pl.ANY)] * 4,
        out_specs=pl.BlockSpec((tm, d), lambda e, m, idx: (e * tiles + m, 0)),
        scratch_shapes=[
            pltpu.VMEM((2, tm, tn.shape[1]), F32),
            pltpu.VMEM((d, ff), F32), pltpu.VMEM((d, ff), F32), pltpu.VMEM((ff, d), F32),
            pltpu.VMEM((d, ff), BF16), pltpu.VMEM((d, ff), BF16), pltpu.VMEM((ff, d), BF16),
            pltpu.SemaphoreType.DMA((2,)), pltpu.SemaphoreType.DMA,
        ],
    )
    idx_padded = jnp.concatenate([idx_flat, idx_flat[:tm]])
    return pl.pallas_call(
        functools.partial(_moe_ffn_kernel, tm=tm, layer=layer),
        grid_spec=grid_spec,
        out_shape=jax.ShapeDtypeStruct((n_exp * slots, d), BF16),
        compiler_params=_params("arbitrary", "arbitrary"),
        name="moe_ffn",
    )(idx_padded, tn, w_gate, w_up, w_down)


def _combine_kernel(src_ref, pieces_ref, dst_ref, chunks_ref, ye_hbm, h_ref, tgt_ref, o_ref, stage, tgt_b, sem,
                    *, static_chunks):
    i = pl.program_id(0)
    n_tiles = pl.num_programs(0)
    n_exp = tgt_ref.shape[1]
    slot = i % 2

    @pl.when(i == 0)
    def _():
        stage[...] = jnp.zeros(stage.shape, stage.dtype)

    def piece(src_row, into, dst_row):
        return pltpu.make_async_copy(ye_hbm.at[pl.ds(src_row, COMBINE_PIECE), :],
                                     stage.at[into, pl.ds(dst_row, COMBINE_PIECE), :], sem.at[into])

    def fetch(tile, into, live):
        n_pieces = 0
        for e in range(n_exp):
            src, dst = src_ref[tile * n_exp + e], dst_ref[tile * n_exp + e]
            n = pieces_ref[tile * n_exp + e] * live

            def issue(p, carry, src=src, dst=dst):
                piece(pl.multiple_of(src + p * COMBINE_PIECE, COMBINE_PIECE), into,
                      pl.multiple_of(dst + p * COMBINE_PIECE, COMBINE_PIECE)).start()
                return carry

            lax.fori_loop(0, n, issue, 0)
            n_pieces = n_pieces + n
        return n_pieces

    @pl.when(i == 0)
    def _():
        fetch(0, 0, 1)

    n_mine = 0
    for e in range(n_exp):
        n_mine = n_mine + pieces_ref[i * n_exp + e]

    def wait(p, carry):
        piece(0, slot, 0).wait()
        return carry

    lax.fori_loop(0, n_mine, wait, 0)
    nxt = jnp.minimum(i + 1, n_tiles - 1)
    fetch(nxt, 1 - slot, jnp.where(i + 1 < n_tiles, 1, 0))
    t_tile = tgt_ref.shape[0]
    for e in range(n_exp):
        tgt_b[e] = jnp.broadcast_to(tgt_ref[:, e:e + 1], (t_tile, LANES))
    lane = lax.broadcasted_iota(jnp.int32, (1, LANES), 1)

    def onehot(base, n_cols):
        parts = []
        for part in range(n_cols // LANES):
            want = lane + (base + part * LANES)
            hit = jnp.zeros((t_tile, LANES), F32)
            for e in range(n_exp):
                hit = jnp.where(tgt_b[e] == want, 1.0, hit)
            parts.append(hit.astype(BF16))
        return jnp.concatenate(parts, axis=1)

    k0 = static_chunks * COMBINE_CHUNK
    o_ref[...] = h_ref[...] + jnp.dot(onehot(0, k0), stage[slot, 0:k0, :], preferred_element_type=F32)

    def chunk(c, carry):
        base = pl.multiple_of(c * COMBINE_CHUNK, COMBINE_CHUNK)
        o_ref[...] += jnp.dot(onehot(base, COMBINE_CHUNK), stage[slot, pl.ds(base, COMBINE_CHUNK), :],
                              preferred_element_type=F32)
        return carry

    lax.fori_loop(static_chunks, chunks_ref[i], chunk, 0)


def _combine(ye, h, sel_t, row_t, t_tile):
    nt, d = h.shape
    n_exp = sel_t.shape[1]
    assert ye.shape[0] % COMBINE_PIECE == 0
    first = row_t[::t_tile]
    last = (row_t + sel_t)[t_tile - 1::t_tile]
    src = first // COMBINE_PIECE * COMBINE_PIECE
    pieces = (last - src + COMBINE_PIECE - 1) // COMBINE_PIECE
    dst = (jnp.cumsum(pieces, axis=1) - pieces) * COMBINE_PIECE
    chunks = (jnp.sum(pieces, axis=1) * COMBINE_PIECE + COMBINE_CHUNK - 1) // COMBINE_CHUNK
    tgt = jnp.where(sel_t > 0, row_t - jnp.repeat(src - dst, t_tile, axis=0), -1)
    stage_rows = pl.cdiv(n_exp * (t_tile + 2 * (COMBINE_PIECE - 1)), COMBINE_CHUNK) * COMBINE_CHUNK
    usual_rows = n_exp * (N_EXPERT_CAPACITY_FACTOR * t_tile // n_exp + COMBINE_PIECE - 1)
    static_chunks = min(pl.cdiv(usual_rows, COMBINE_CHUNK), stage_rows // COMBINE_CHUNK)
    grid_spec = pltpu.PrefetchScalarGridSpec(
        num_scalar_prefetch=4,
        grid=(nt // t_tile,),
        in_specs=[
            pl.BlockSpec(memory_space=pl.ANY),
            pl.BlockSpec((t_tile, d), lambda i, *_: (i, 0)),
            pl.BlockSpec((t_tile, n_exp), lambda i, *_: (i, 0)),
        ],
        out_specs=pl.BlockSpec((t_tile, d), lambda i, *_: (i, 0)),
        scratch_shapes=[pltpu.VMEM((2, stage_rows, d), ye.dtype), pltpu.VMEM((n_exp, t_tile, LANES), jnp.int32),
                        pltpu.SemaphoreType.DMA((2,))],
    )
    flat = lambda x: x.reshape(-1).astype(jnp.int32)
    return pl.pallas_call(
        functools.partial(_combine_kernel, static_chunks=static_chunks),
        grid_spec=grid_spec,
        out_shape=jax.ShapeDtypeStruct((nt, d), F32),
        compiler_params=_params("arbitrary"),
        name="moe_combine",
    )(flat(src), flat(pieces), flat(dst), flat(chunks), ye, h, tgt.astype(jnp.int32))


def _threshold_kernel(x_ref, lo_ref, hi_ref, *, cap):
    n_exp = x_ref.shape[0]
    cap_f = float(cap)

    def enough(e, threshold):
        return jnp.sum(jnp.where(x_ref[e] >= threshold, 1.0, 0.0), axis=(0, 1), keepdims=True) >= cap_f

    def as_f32(bits):
        return lax.bitcast_convert_type(bits, F32)

    def bisect_bits(_, bounds):
        new = []
        for e in range(n_exp):
            lo, hi = bounds[e]
            mid = lo + ((hi - lo) >> 1)
            ok = enough(e, as_f32(mid))
            new.append((jnp.where(ok, mid, lo), jnp.where(ok, hi, mid)))
        return tuple(new)

    start = (jnp.zeros((1, 1), jnp.int32), jnp.full((1, 1), F32_INF_BITS, jnp.int32))
    bounds = lax.fori_loop(0, 31, bisect_bits, tuple(start for _ in range(n_exp)))

    def bisect_values(_, bounds):
        new = []
        for e in range(n_exp):
            lo, hi = bounds[e]
            mid = lo + (hi - lo) * 0.5
            ok = enough(e, mid)
            new.append((jnp.where(ok, mid, lo), jnp.where(ok, hi, mid)))
        return tuple(new)

    bounds = lax.fori_loop(0, ROUTE_REFINE_STEPS, bisect_values, tuple((as_f32(lo), as_f32(hi)) for lo, hi in bounds))
    for e in range(n_exp):
        lo_ref[e] = jnp.broadcast_to(bounds[e][0], lo_ref.shape[1:])
        hi_ref[e] = jnp.broadcast_to(bounds[e][1], hi_ref.shape[1:])


def _route_kernel(x_ref, lo_ref, hi_ref, sel_ref, pexc_ref, idx_ref, *, cap, slot_block):
    x = x_ref[0]
    n_rows = x.shape[0]
    cap_f = float(cap)
    lo, hi = lo_ref[0, 0:1, 0:1], hi_ref[0, 0:1, 0:1]

    upper = jnp.where(lax.broadcasted_iota(jnp.int32, (LANES, LANES), 0)
                      <= lax.broadcasted_iota(jnp.int32, (LANES, LANES), 1), 1.0, 0.0).astype(BF16)
    lower = jnp.where(lax.broadcasted_iota(jnp.int32, (n_rows, n_rows), 1)
                      < lax.broadcasted_iota(jnp.int32, (n_rows, n_rows), 0), 1.0, 0.0).astype(BF16)

    def prefix(m):
        within = jnp.dot(m.astype(BF16), upper, preferred_element_type=F32)
        row_tot = jnp.broadcast_to(within[:, LANES - 1:], within.shape)
        before = jnp.dot(lower, row_tot.astype(BF16), preferred_element_type=F32)
        return within, before, row_tot

    gt = jnp.where(x >= hi, 1.0, 0.0)
    eq = jnp.where(x >= lo, 1.0, 0.0) - gt
    need = cap_f - jnp.sum(gt, axis=(0, 1), keepdims=True)
    tie_within, tie_before, _ = prefix(eq)
    tie_rank = tie_within + tie_before - eq
    sel = gt + eq * jnp.where(tie_rank < need, 1.0, 0.0)
    within, before, row_tot = prefix(sel)
    sel_ref[0] = sel.astype(jnp.int32)
    pexc_ref[0] = (within + before - sel).astype(jnp.int32)

    row_end = jnp.transpose(before + row_tot)[0:1, :]
    row_start = jnp.transpose(before)[0:1, :]
    within_b = within.astype(BF16)
    row_id = lax.broadcasted_iota(jnp.int32, (1, n_rows), 1).astype(F32)
    for s in range(idx_ref.shape[1] // slot_block):
        j = (lax.broadcasted_iota(jnp.int32, (slot_block, 1), 0) + s * slot_block).astype(F32)
        row = jnp.sum(jnp.where(row_end <= j, 1.0, 0.0), axis=-1, keepdims=True)
        onehot = jnp.where(row_id == row, 1.0, 0.0)
        cum = jnp.dot(onehot.astype(BF16), within_b, preferred_element_type=F32)
        local = j - jnp.sum(onehot * row_start, axis=-1, keepdims=True)
        col = jnp.sum(jnp.where(cum <= local, 1.0, 0.0), axis=-1, keepdims=True)
        tok = (row * LANES + col).astype(jnp.int32)
        idx_ref[0, s * slot_block:(s + 1) * slot_block, :] = jnp.where(j < cap_f, tok, 0)


def _route(aff_rows, cap):
    n_exp, n_rows, _ = aff_rows.shape
    cap_pad = pl.cdiv(cap, ROUTE_SLOT_BLOCK) * ROUTE_SLOT_BLOCK
    blk = pl.BlockSpec((1, n_rows, LANES), lambda e: (e, 0, 0))
    bound = jax.ShapeDtypeStruct((n_exp, 8, LANES), F32)
    bound_blk = pl.BlockSpec((1, 8, LANES), lambda e: (e, 0, 0))
    lo, hi = pl.pallas_call(
        functools.partial(_threshold_kernel, cap=cap),
        out_shape=[bound, bound],
        compiler_params=pltpu.CompilerParams(vmem_limit_bytes=VMEM_LIMIT_BYTES),
        name="route_threshold",
    )(aff_rows)
    return pl.pallas_call(
        functools.partial(_route_kernel, cap=cap, slot_block=ROUTE_SLOT_BLOCK),
        grid=(n_exp,),
        in_specs=[blk, bound_blk, bound_blk],
        out_specs=[blk, blk, pl.BlockSpec((1, cap_pad, 1), lambda e: (e, 0, 0))],
        out_shape=[
            jax.ShapeDtypeStruct(aff_rows.shape, jnp.int32),
            jax.ShapeDtypeStruct(aff_rows.shape, jnp.int32),
            jax.ShapeDtypeStruct((n_exp, cap_pad, 1), jnp.int32),
        ],
        compiler_params=_params("parallel"),
        name="route",
    )(aff_rows, lo, hi)


def _moe(h, tn, aff, group_sizes, w_gate, w_up, w_down, layer):
    n_exp = aff.shape[1]
    caps = [N_EXPERT_CAPACITY_FACTOR * n // n_exp for n in group_sizes]
    slots = pl.cdiv(sum(caps), FFN_SLOT_ALIGN) * FFN_SLOT_ALIGN
    idx_parts, sel_parts, row_parts = [], [], []
    tok0, slot0 = 0, 0
    for n, cap in zip(group_sizes, caps):
        n_rows = pl.cdiv(pl.cdiv(n, LANES), LANES) * LANES
        rows = jnp.pad(aff[tok0: tok0 + n].T, ((0, 0), (0, n_rows * LANES - n)), constant_values=-1.0)
        sel, pexc, idx = _route(rows.reshape(n_exp, n_rows, LANES), cap)
        idx_parts.append(idx[:, :cap, 0] + tok0)
        sel_parts.append(sel.reshape(n_exp, -1)[:, :n].T)
        expert_row0 = jnp.arange(n_exp, dtype=jnp.int32)[None, :] * slots + slot0
        row_parts.append(pexc.reshape(n_exp, -1)[:, :n].T + expert_row0)
        tok0 += n
        slot0 += cap
    idx_parts.append(jnp.zeros((n_exp, slots - sum(caps)), jnp.int32))
    idx_all = jnp.concatenate(idx_parts, axis=1)
    sel_t = jnp.concatenate(sel_parts, axis=0)
    row_t = jnp.concatenate(row_parts, axis=0)
    ye = _moe_ffn(idx_all.reshape(-1), tn, w_gate, w_up, w_down, layer, slots)
    t_tile = _tile(math.gcd(*group_sizes), COMBINE_TOKENS_MAX, 8)
    return _combine(ye, h, sel_t, row_t, t_tile)


def _final_norm_kernel(x_ref, g_ref, o_ref):
    o_ref[...] = _rms(x_ref[...], g_ref[...])


def _final_norm(h, g, first_seq, n_seqs, seq):
    _, d = h.shape
    s_out = seq - N_META
    tm = _tile(s_out, 512, 8)
    per_seq = s_out // tm
    out = pl.pallas_call(
        _final_norm_kernel,
        grid=(n_seqs, per_seq),
        in_specs=[
            pl.BlockSpec((pl.Element(tm), pl.Element(d)),
                         lambda b, j: (pl.multiple_of((first_seq + b) * seq + N_META + j * tm, 8), 0)),
            pl.BlockSpec((1, d), lambda b, j: (0, 0)),
        ],
        out_specs=pl.BlockSpec((tm, d), lambda b, j: (b * per_seq + j, 0)),
        out_shape=jax.ShapeDtypeStruct((n_seqs * s_out, d), F32),
        compiler_params=_params("parallel", "parallel"),
        name="final_norm",
    )(h, g)
    return out.reshape(n_seqs, s_out, d)


def _lambda_init(layer):
    return 0.8 - 0.6 * math.exp(-0.3 * layer)


def kernel(x_prompt, x_sample, meta_tokens, mix_norm_g, ffn_norm_g, final_norm_g, ev_w_in, ev_w_out, ev_lambda_q1, ev_lambda_k1, ev_lambda_q2, ev_lambda_k2, ev_subln_g, ev_conv_w, od_w_in, od_q_norm_g, od_w_q_b, od_kv_norm_g, od_w_kv_b, od_w_out, moe_w_router, moe_w_gate, moe_w_up, moe_w_down):
    d = x_prompt.shape[-1]
    seq = N_META + x_prompt.shape[1]
    assert x_sample.shape[1] == x_prompt.shape[1]
    nb_p, nb_s = x_prompt.shape[0], x_sample.shape[0]
    nb = nb_p + nb_s
    nt = nb * seq
    group_sizes = (nb_p * seq, nb_s * seq)
    depth = mix_norm_g.shape[0]

    x = jnp.concatenate([x_prompt, x_sample], axis=0)
    meta = jnp.broadcast_to(meta_tokens[None].astype(x.dtype), (nb, N_META, d))
    h = jnp.concatenate([meta, x], axis=1).reshape(nt, d)
    cos128, sin128 = _rope_tables(seq)

    for layer in range(depth):
        i = layer // 2
        mix_g = mix_norm_g[layer][None]
        ffn_g = ffn_norm_g[layer][None]
        w_router = moe_w_router[layer].astype(BF16)
        if layer % 2 == 0:
            width = d // 2
            heads = width // LANES
            w_in = ev_w_in[i].astype(BF16)
            qk3 = _even_proj(h, mix_g, w_in, cos128, sin128, seq, True).reshape(nb, seq, 2 * width)
            rest3 = _even_proj(h, mix_g, w_in, cos128, sin128, seq, False).reshape(nb, seq, 4 * width)
            attn = _diff_attn(qk3, rest3, ev_lambda_q1[i][None], ev_lambda_k1[i][None], ev_lambda_q2[i][None],
                              ev_lambda_k2[i][None], ev_subln_g[i][None], _lambda_init(layer), heads)
            conv = _gated_conv(rest3, ev_conv_w[i], width)
            h, tn, aff = _out_proj_router(attn.reshape(nt, width), conv.reshape(nt, width), 0, 0,
                                          ev_w_out[i].astype(BF16), h, ffn_g, w_router)
        else:
            q_rank, kv_rank = od_q_norm_g.shape[1], od_kv_norm_g.shape[1]
            heads = od_w_q_b.shape[2] // (MLA_NOPE + MLA_ROPE)
            w_in_pad = jnp.pad(od_w_in[i], ((0, 0), (0, LANES - MLA_ROPE))).astype(BF16)
            w_q = od_w_q_b[i].reshape(q_rank, heads, MLA_NOPE + MLA_ROPE)
            w_q = jnp.pad(w_q, ((0, 0), (0, 0), (0, MLA_QK_PAD - MLA_NOPE - MLA_ROPE)))
            w_q = w_q.reshape(q_rank, heads * MLA_QK_PAD).astype(BF16)
            w_kv = od_w_kv_b[i].reshape(kv_rank, heads, MLA_NOPE + MLA_V)
            w_kv = jnp.concatenate([w_kv[:, :, :MLA_NOPE].reshape(kv_rank, heads * MLA_NOPE),
                                    w_kv[:, :, MLA_NOPE:].reshape(kv_rank, heads * MLA_V)], axis=1).astype(BF16)
            cq, ckv, kr = _odd_in(h, mix_g, w_in_pad, od_q_norm_g[i][None], od_kv_norm_g[i][None], cos128, sin128, seq)
            q = _up_proj(cq, w_q, cos128, sin128, seq, True, (MLA_NOPE + MLA_ROPE) ** -0.5 * LOG2E)
            kv = _up_proj(ckv, w_kv, cos128, sin128, seq, False, 1.0)
            o = _mla_attn(q.reshape(nb, seq, heads * MLA_QK_PAD), kv.reshape(nb, seq, 2 * heads * LANES),
                          kr.reshape(nb, seq, LANES), heads)
            o = o.reshape(nt, heads * MLA_V)
            h, tn, aff = _out_proj_router(o, o, 0, 1, od_w_out[i].astype(BF16), h, ffn_g, w_router)
        h = _moe(h, tn, aff, group_sizes, moe_w_gate, moe_w_up, moe_w_down, layer)

    g = final_norm_g[None]
    return (_final_norm(h, g, 0, nb_p, seq), _final_norm(h, g, nb_p, nb_s, seq))
```

```python
import functools
import math

import jax
import jax.numpy as jnp
from jax import lax
from jax.experimental import pallas as pl
from jax.experimental.pallas import tpu as pltpu

N_META = 16
EPS = 1e-6
ROPE_THETA = 10000.0
HEAD_DIM = 64
LANES = 128
MXU_COLS = 256
MLA_NOPE = 128
MLA_ROPE = 64
MLA_V = 128
MLA_QK_PAD = 256
ATTN_Q_TILE_MAX = 768
ATTN_ROW_BLOCK = 16
MLA_HEADS_PER_STEP = 2
DIFF_HEADS_PER_STEP = 1
N_EXPERT_CAPACITY_FACTOR = 2
VMEM_LIMIT_BYTES = 56 * 1024 * 1024
NEG_BIG = -1e30
F32_INF_BITS = 0x7F800000
LOG2E = math.log2(math.e)
ROUTE_SLOT_BLOCK = 512
ROUTE_REFINE_STEPS = 24
COMBINE_TOKENS_MAX = 192
COMBINE_PIECE = 16
FFN_ROWS_MAX = 448
FFN_SLOT_ALIGN = 64
WEIGHT_CAST_ROWS = 128
COMBINE_CHUNK = 256

F32 = jnp.float32
BF16 = jnp.bfloat16


def _tile(n, cap, mult):
    best = None
    for t in range(mult, min(n, cap) + 1, mult):
        if n % t == 0:
            best = t
    assert best is not None, (n, cap, mult)
    return best


def _params(*sem):
    return pltpu.CompilerParams(dimension_semantics=sem, vmem_limit_bytes=VMEM_LIMIT_BYTES)


def _rms(x, g):
    return x * lax.rsqrt(jnp.mean(x * x, axis=-1, keepdims=True) + EPS) * g


def _rope128(x, cos, sin_signed):
    lane = lax.broadcasted_iota(jnp.int32, x.shape, 1)
    first_half = (lane % HEAD_DIM) < (HEAD_DIM // 2)
    rot = jnp.where(first_half, pltpu.roll(x, LANES - HEAD_DIM // 2, 1), pltpu.roll(x, HEAD_DIM // 2, 1))
    return x * cos + rot * sin_signed


def _rope_tables(length):
    inv = 1.0 / (ROPE_THETA ** (jnp.arange(0, HEAD_DIM, 2, dtype=F32) / HEAD_DIM))
    ang = jnp.arange(length, dtype=F32)[:, None] * inv[None, :]
    cos, sin = jnp.cos(ang), jnp.sin(ang)
    cos128 = jnp.tile(cos, (1, LANES // (HEAD_DIM // 2)))
    sin128 = jnp.tile(jnp.concatenate([-sin, sin], axis=1), (1, LANES // HEAD_DIM))
    return cos128, sin128


def _even_proj_kernel(x_ref, g_ref, w_ref, cos_ref, sin_ref, o_ref, xn_ref, *, rope, q_scale):
    j = pl.program_id(1)

    @pl.when(j == 0)
    def _():
        xn_ref[...] = _rms(x_ref[...], g_ref[...]).astype(BF16)

    xn = xn_ref[...]
    if rope:
        scale = jnp.where(j == 0, q_scale, 1.0).astype(F32)
        cos, sin = cos_ref[...], sin_ref[...]
    for c in range(o_ref.shape[1] // MXU_COLS):
        lo = c * MXU_COLS
        acc = jnp.dot(xn, w_ref[:, lo: lo + MXU_COLS], preferred_element_type=F32)
        if rope:
            for half in range(MXU_COLS // LANES):
                sl = slice(half * LANES, (half + 1) * LANES)
                o_ref[:, lo + half * LANES: lo + (half + 1) * LANES] = (
                    _rope128(acc[:, sl], cos, sin) * scale).astype(o_ref.dtype)
        else:
            o_ref[:, lo: lo + MXU_COLS] = acc.astype(o_ref.dtype)


def _even_proj(h, g, w_in, cos128, sin128, seq, rope):
    nt, d = h.shape
    tn = w_in.shape[1] // 6
    col0, n_cols = (0, 2) if rope else (2, 4)
    tm = _tile(seq, 768, 16)
    tiles_per_seq = seq // tm
    return pl.pallas_call(
        functools.partial(_even_proj_kernel, rope=rope, q_scale=HEAD_DIM ** -0.5 * LOG2E),
        grid=(nt // tm, n_cols),
        in_specs=[
            pl.BlockSpec((tm, d), lambda i, j: (i, 0)),
            pl.BlockSpec((1, d), lambda i, j: (0, 0)),
            pl.BlockSpec((d, tn), lambda i, j: (0, col0 + j)),
            pl.BlockSpec((tm, LANES), lambda i, j: (i % tiles_per_seq, 0)),
            pl.BlockSpec((tm, LANES), lambda i, j: (i % tiles_per_seq, 0)),
        ],
        out_specs=pl.BlockSpec((tm, tn), lambda i, j: (i, j)),
        out_shape=jax.ShapeDtypeStruct((nt, n_cols * tn), BF16),
        scratch_shapes=[pltpu.VMEM((tm, d), BF16)],
        compiler_params=_params("parallel", "arbitrary"),
        name="even_proj_qk" if rope else "even_proj_rest",
    )(h, g, w_in, cos128, sin128)


def _store_scores(q, k_ref, tail_bias, s_ref):
    s = _nt_dot(q, k_ref[...])
    n_main = s.shape[1] - LANES
    s_ref[:, :n_main] = s[:, :n_main]
    s_ref[:, n_main:] = s[:, n_main:] + tail_bias


def _probs(s):
    p = jnp.exp2(s - jnp.max(s, axis=-1, keepdims=True))
    return p, jnp.sum(p, axis=-1, keepdims=True)


def _row_blocks(n_rows):
    return [slice(r, r + ATTN_ROW_BLOCK) for r in range(0, n_rows, ATTN_ROW_BLOCK)]


def _tail_bias(seq, lp):
    lane = lax.broadcasted_iota(jnp.int32, (1, LANES), 1)
    return jnp.where(lane < seq - (lp - LANES), 0.0, NEG_BIG).astype(F32)


def _nt_dot(a, b):
    return lax.dot_general(a, b, (((1,), (1,)), ((), ())), preferred_element_type=F32)


def _attention_pipeline(n_tiles, scores, softmax, pv):
    sums = {}
    for step in range(n_tiles + 2):
        if 0 <= step - 2 < n_tiles:
            pv(step - 2, (step - 2) % 2, sums.pop(step - 2))
        if 0 <= step - 1 < n_tiles:
            sums[step - 1] = softmax((step - 1) % 2)
        if step < n_tiles:
            scores(step, step % 2)


def _diff_attn_kernel(lq1_ref, lk1_ref, lq2_ref, lk2_ref, q_ref, k_ref, v_ref, g_ref, o_ref, kpad, vpad,
                      s1a, s1b, s2a, s2b, pa, pb, *, seq, lam_init, tq):
    n_heads, lp = kpad.shape[0], kpad.shape[1]
    for h in range(n_heads):
        cols = slice(h * LANES, (h + 1) * LANES)
        kpad[h, 0:seq, :] = k_ref[0, :, cols]
        kpad[h, seq:lp, :] = jnp.zeros((lp - seq, LANES), BF16)
        vpad[h, 0:seq, :] = v_ref[0, :, cols]
        vpad[h, seq:lp, :] = jnp.zeros((lp - seq, LANES), BF16)
    lam = (jnp.exp(jnp.sum(lq1_ref[...] * lk1_ref[...], axis=-1, keepdims=True))
           - jnp.exp(jnp.sum(lq2_ref[...] * lk2_ref[...], axis=-1, keepdims=True)) + lam_init)
    tail_bias = _tail_bias(seq, lp)
    gain = g_ref[...] * (1.0 - lam_init)
    per_head = seq // tq
    s_bufs, p_bufs = ((s1a, s2a), (s1b, s2b)), (pa, pb)

    def scores(t, par):
        h, r = divmod(t, per_head)
        q = q_ref[0, r * tq:(r + 1) * tq, h * LANES:(h + 1) * LANES]
        lane = lax.broadcasted_iota(jnp.int32, q.shape, 1)
        zero = jnp.zeros_like(q)
        _store_scores(jnp.where(lane < HEAD_DIM, q, zero), kpad.at[h], tail_bias, s_bufs[par][0])
        _store_scores(jnp.where(lane >= HEAD_DIM, q, zero), kpad.at[h], tail_bias, s_bufs[par][1])

    def softmax(par):
        for rows in _row_blocks(tq):
            p1, l1 = _probs(s_bufs[par][0][rows, :])
            p2, l2 = _probs(s_bufs[par][1][rows, :])
            p_bufs[par][rows, :] = (p1 * (1.0 / l1) - p2 * (lam / l2)).astype(BF16)

    def pv(t, par, _):
        h, r = divmod(t, per_head)
        o = jnp.dot(p_bufs[par][...], vpad[h], preferred_element_type=F32)
        o = o * lax.rsqrt(jnp.mean(o * o, axis=-1, keepdims=True) + EPS) * gain
        o_ref[0, r * tq:(r + 1) * tq, h * LANES:(h + 1) * LANES] = o.astype(o_ref.dtype)

    _attention_pipeline(n_heads * per_head, scores, softmax, pv)


def _diff_attn(qk3, rest3, lq1, lk1, lq2, lk2, subln_g, lam_init, heads):
    nb, seq, _ = qk3.shape
    lp = pl.cdiv(seq, LANES) * LANES
    assert lp > seq
    tq = _tile(seq, ATTN_Q_TILE_MAX, 16)
    vec = lambda: pl.BlockSpec((1, HEAD_DIM), lambda b, h: (0, 0))
    hs = DIFF_HEADS_PER_STEP
    assert heads % hs == 0
    return pl.pallas_call(
        functools.partial(_diff_attn_kernel, seq=seq, lam_init=lam_init, tq=tq),
        grid=(nb, heads // hs),
        in_specs=[
            vec(), vec(), vec(), vec(),
            pl.BlockSpec((1, seq, hs * LANES), lambda b, h: (b, 0, h)),
            pl.BlockSpec((1, seq, hs * LANES), lambda b, h: (b, 0, heads // hs + h)),
            pl.BlockSpec((1, seq, hs * LANES), lambda b, h: (b, 0, h)),
            pl.BlockSpec((1, LANES), lambda b, h: (0, 0)),
        ],
        out_specs=pl.BlockSpec((1, seq, hs * LANES), lambda b, h: (b, 0, h)),
        out_shape=jax.ShapeDtypeStruct((nb, seq, heads * LANES), BF16),
        scratch_shapes=[pltpu.VMEM((hs, lp, LANES), BF16), pltpu.VMEM((hs, lp, LANES), BF16)]
        + [pltpu.VMEM((tq, lp), F32)] * 4 + [pltpu.VMEM((tq, lp), BF16)] * 2,
        compiler_params=_params("parallel", "parallel"),
        name="diff_attn",
    )(lq1, lk1, lq2, lk2, qk3, qk3, rest3, subln_g)


def _conv_kernel(gb_ref, gc_ref, cx_ref, w_ref, o_ref):
    u = gc_ref[0].astype(F32) * cx_ref[0].astype(F32)
    seq = u.shape[0]
    row = lax.broadcasted_iota(jnp.int32, u.shape, 0)
    prev = jnp.where(row == 0, 0.0, pltpu.roll(u, 1, 0))
    nxt = jnp.where(row == seq - 1, 0.0, pltpu.roll(u, seq - 1, 0))
    y = prev * w_ref[0:1, :] + u * w_ref[1:2, :] + nxt * w_ref[2:3, :]
    o_ref[0] = (gb_ref[0].astype(F32) * y).astype(o_ref.dtype)


def _gated_conv(rest3, conv_w, width):
    nb, seq, _ = rest3.shape
    tc = _tile(width, 256, LANES)
    nc = width // tc
    blk = lambda part: pl.BlockSpec((1, seq, tc), lambda b, c: (b, 0, part * nc + c))
    return pl.pallas_call(
        _conv_kernel,
        grid=(nb, nc),
        in_specs=[blk(1), blk(2), blk(3), pl.BlockSpec((3, tc), lambda b, c: (0, c))],
        out_specs=pl.BlockSpec((1, seq, tc), lambda b, c: (b, 0, c)),
        out_shape=jax.ShapeDtypeStruct((nb, seq, width), BF16),
        compiler_params=_params("parallel", "parallel"),
        name="gated_conv",
    )(rest3, rest3, rest3, conv_w)


def _out_proj_router_kernel(a1_ref, a2_ref, w_ref, h_ref, g_ref, wr_ref, hn_ref, tn_ref, aff_ref):
    k1 = a1_ref.shape[1]
    y = jnp.dot(a1_ref[...], w_ref[0:k1, :], preferred_element_type=F32)
    y = y + jnp.dot(a2_ref[...], w_ref[k1:, :], preferred_element_type=F32)
    h = h_ref[...] + y
    hn_ref[...] = h
    t = _rms(h, g_ref[...])
    d = t.shape[1]
    n_exp = aff_ref.shape[1]
    logits = jnp.dot(t.astype(BF16), wr_ref[...], preferred_element_type=F32)
    lane = lax.broadcasted_iota(jnp.int32, logits.shape, 1)
    logits = jnp.where(lane < n_exp, logits, NEG_BIG)
    e = jnp.exp(logits - jnp.max(logits, axis=-1, keepdims=True))
    aff = e / jnp.sum(e, axis=-1, keepdims=True)
    aff_ref[...] = aff[:, :n_exp]
    tn_ref[:, :d] = t
    tn_ref[:, d:] = aff


def _out_proj_router(a1, a2, col1, col2, w_out, h, ffn_g, w_router):
    nt, d = h.shape
    k1 = w_out.shape[0] // 2
    n_exp = w_router.shape[1]
    w_router = jnp.pad(w_router, ((0, 0), (0, LANES - n_exp)))
    tm = _tile(nt, 512, 16)
    row = lambda c: (lambda i: (i, c))
    const = lambda i: (0, 0)
    return pl.pallas_call(
        _out_proj_router_kernel,
        grid=(nt // tm,),
        in_specs=[
            pl.BlockSpec((tm, k1), row(col1)),
            pl.BlockSpec((tm, k1), row(col2)),
            pl.BlockSpec(w_out.shape, const),
            pl.BlockSpec((tm, d), row(0)),
            pl.BlockSpec((1, d), const),
            pl.BlockSpec(w_router.shape, const),
        ],
        out_specs=[
            pl.BlockSpec((tm, d), row(0)),
            pl.BlockSpec((tm, d + LANES), row(0)),
            pl.BlockSpec((tm, n_exp), row(0)),
        ],
        out_shape=[
            jax.ShapeDtypeStruct((nt, d), F32),
            jax.ShapeDtypeStruct((nt, d + LANES), F32),
            jax.ShapeDtypeStruct((nt, n_exp), F32),
        ],
        compiler_params=_params("parallel"),
        name="out_proj_router",
    )(a1, a2, w_out, h, ffn_g, w_router)


def _odd_proj_kernel(x_ref, g_ref, w_ref, qg_ref, kvg_ref, wq_ref, wkv_ref, cos_ref, sin_ref, q_ref, kv_ref, kr_ref,
                     *, scale):
    xn = _rms(x_ref[...], g_ref[...]).astype(BF16)
    p = jnp.dot(xn, w_ref[...], preferred_element_type=F32)
    qr, kvr = qg_ref.shape[1], kvg_ref.shape[1]
    cq = _rms(p[:, :qr], qg_ref[...]).astype(BF16)
    ckv = _rms(p[:, qr: qr + kvr], kvg_ref[...]).astype(BF16)
    cos, sin = cos_ref[...], sin_ref[...]
    kr_ref[...] = _rope128(p[:, qr + kvr:], cos, sin).astype(kr_ref.dtype)
    for c in range(q_ref.shape[1] // MLA_QK_PAD):
        lo = c * MLA_QK_PAD
        acc = jnp.dot(cq, wq_ref[:, lo: lo + MLA_QK_PAD], preferred_element_type=F32)
        q_ref[:, lo: lo + LANES] = (acc[:, :LANES] * scale).astype(q_ref.dtype)
        q_ref[:, lo + LANES: lo + MLA_QK_PAD] = (_rope128(acc[:, LANES:], cos, sin) * scale).astype(q_ref.dtype)
    for c in range(kv_ref.shape[1] // MXU_COLS):
        cols = slice(c * MXU_COLS, (c + 1) * MXU_COLS)
        kv_ref[:, cols] = jnp.dot(ckv, wkv_ref[:, cols], preferred_element_type=F32).astype(kv_ref.dtype)


def _odd_proj(h, g, w_in_pad, q_g, kv_g, w_q, w_kv, cos128, sin128, seq, scale):
    nt, d = h.shape
    tm = _tile(seq, 768, 16)
    tiles_per_seq = seq // tm
    row = lambda i: (i, 0)
    const = lambda i: (0, 0)
    pos = lambda i: (i % tiles_per_seq, 0)
    fixed = lambda a: pl.BlockSpec(a.shape, const, pipeline_mode=pl.Buffered(1))
    return pl.pallas_call(
        functools.partial(_odd_proj_kernel, scale=scale),
        grid=(nt // tm,),
        in_specs=[
            pl.BlockSpec((tm, d), row),
            fixed(g), fixed(w_in_pad), fixed(q_g), fixed(kv_g), fixed(w_q), fixed(w_kv),
            pl.BlockSpec((tm, LANES), pos),
            pl.BlockSpec((tm, LANES), pos),
        ],
        out_specs=[
            pl.BlockSpec((tm, w_q.shape[1]), row),
            pl.BlockSpec((tm, w_kv.shape[1]), row),
            pl.BlockSpec((tm, LANES), row),
        ],
        out_shape=[
            jax.ShapeDtypeStruct((nt, w_q.shape[1]), BF16),
            jax.ShapeDtypeStruct((nt, w_kv.shape[1]), BF16),
            jax.ShapeDtypeStruct((nt, LANES), BF16),
        ],
        compiler_params=_params("parallel"),
        name="odd_proj",
    )(h, g, w_in_pad, q_g, kv_g, w_q, w_kv, cos128, sin128)


def _mla_attn_kernel(q_ref, kn_ref, kr_ref, v_ref, o_ref, kcat, vpad, sa, sb, pa, pb, la, lb, *, seq, tq):
    n_heads, lp = kcat.shape[0], kcat.shape[1]
    for h in range(n_heads):
        kcat[h, 0:seq, 0:LANES] = kn_ref[0, :, h * LANES:(h + 1) * LANES]
        kcat[h, 0:seq, LANES:] = kr_ref[0]
        kcat[h, seq:lp, :] = jnp.zeros((lp - seq, MLA_QK_PAD), BF16)
        vpad[h, 0:seq, :] = v_ref[0, :, h * LANES:(h + 1) * LANES]
        vpad[h, seq:lp, :] = jnp.zeros((lp - seq, LANES), BF16)
    tail_bias = _tail_bias(seq, lp)
    per_head = seq // tq
    s_bufs, p_bufs, l_bufs = (sa, sb), (pa, pb), (la, lb)

    def scores(t, par):
        h, r = divmod(t, per_head)
        _store_scores(q_ref[0, r * tq:(r + 1) * tq, h * MLA_QK_PAD:(h + 1) * MLA_QK_PAD], kcat.at[h], tail_bias,
                      s_bufs[par])

    def softmax(par):
        for rows in _row_blocks(tq):
            p, l = _probs(s_bufs[par][rows, :])
            p_bufs[par][rows, :] = p.astype(BF16)
            l_bufs[par][rows, :] = l

    def pv(t, par, _):
        h, r = divmod(t, per_head)
        o = jnp.dot(p_bufs[par][...], vpad[h], preferred_element_type=F32) / l_bufs[par][...]
        o_ref[0, r * tq:(r + 1) * tq, h * LANES:(h + 1) * LANES] = o.astype(o_ref.dtype)

    _attention_pipeline(n_heads * per_head, scores, softmax, pv)


def _mla_attn(q3, kv3, kr3, heads):
    nb, seq, _ = q3.shape
    lp = pl.cdiv(seq, LANES) * LANES
    assert lp > seq
    tq = _tile(seq, ATTN_Q_TILE_MAX, 16)
    hs = MLA_HEADS_PER_STEP
    assert heads % hs == 0
    return pl.pallas_call(
        functools.partial(_mla_attn_kernel, seq=seq, tq=tq),
        grid=(nb, heads // hs),
        in_specs=[
            pl.BlockSpec((1, seq, hs * MLA_QK_PAD), lambda b, h: (b, 0, h)),
            pl.BlockSpec((1, seq, hs * LANES), lambda b, h: (b, 0, h)),
            pl.BlockSpec((1, seq, LANES), lambda b, h: (b, 0, 0)),
            pl.BlockSpec((1, seq, hs * LANES), lambda b, h: (b, 0, heads // hs + h)),
        ],
        out_specs=pl.BlockSpec((1, seq, hs * LANES), lambda b, h: (b, 0, h)),
        out_shape=jax.ShapeDtypeStruct((nb, seq, heads * LANES), BF16),
        scratch_shapes=[pltpu.VMEM((hs, lp, MLA_QK_PAD), BF16), pltpu.VMEM((hs, lp, LANES), BF16)]
        + [pltpu.VMEM((tq, lp), F32)] * 2 + [pltpu.VMEM((tq, lp), BF16)] * 2 + [pltpu.VMEM((tq, 1), F32)] * 2,
        compiler_params=_params("parallel", "parallel"),
        name="mla_attn",
    )(q3, kv3, kr3, kv3)


def _moe_ffn_kernel(idx_ref, tn_hbm, wg_hbm, wu_hbm, wd_hbm, o_ref, xs_buf, wg32, wu32, wd32, wg_ref, wu_ref, wd_ref,
                    sem, wsem, *, tm, layer):
    expert, n_exp = pl.program_id(0), pl.num_programs(0)
    step = expert * pl.num_programs(1) + pl.program_id(1)
    n_steps = n_exp * pl.num_programs(1)
    slot = step % 2
    staged = ((wg_hbm, wg32, wg_ref), (wu_hbm, wu32, wu_ref), (wd_hbm, wd32, wd_ref))

    def weight_copies(e):
        return [pltpu.make_async_copy(src.at[layer, e], stage, wsem) for src, stage, _ in staged]

    @pl.when(step == 0)
    def _():
        for copy in weight_copies(0):
            copy.start()

    @pl.when(pl.program_id(1) == 0)
    def _():
        for copy in weight_copies(expert):
            copy.wait()
        for _, stage, w_ref in staged:
            def cast_rows(c, carry, stage=stage, w_ref=w_ref):
                rows = pl.ds(pl.multiple_of(c * WEIGHT_CAST_ROWS, WEIGHT_CAST_ROWS), WEIGHT_CAST_ROWS)
                w_ref[rows, :] = stage[rows, :].astype(BF16)
                return carry

            lax.fori_loop(0, stage.shape[0] // WEIGHT_CAST_ROWS, cast_rows, 0)

        @pl.when(expert + 1 < n_exp)
        def _():
            for copy in weight_copies(expert + 1):
                copy.start(priority=1)


    def gather(tile, into):
        for r in range(tm):
            tok = idx_ref[tile * tm + r]
            pltpu.make_async_copy(tn_hbm.at[pl.ds(tok, 1), :], xs_buf.at[into, pl.ds(r, 1), :], sem.at[into]).start()

    def wait(which):
        pltpu.make_async_copy(tn_hbm.at[pl.ds(0, tm), :], xs_buf.at[which], sem.at[which]).wait()

    @pl.when(step == 0)
    def _():
        gather(0, 0)

    wait(slot)
    d = o_ref.shape[1]
    xs = xs_buf[slot, :, :d].astype(BF16)
    aff = xs_buf[slot, :, d:]
    lane = lax.broadcasted_iota(jnp.int32, aff.shape, 1)
    gate = jnp.sum(jnp.where(lane == pl.program_id(0), aff, 0.0), axis=-1, keepdims=True)
    hg = jnp.dot(xs, wg_ref[...], preferred_element_type=F32)
    hu = jnp.dot(xs, wu_ref[...], preferred_element_type=F32)
    hid = (hg * jax.nn.sigmoid(hg) * hu).astype(BF16)
    gather(step + 1, 1 - slot)
    o_ref[...] = (jnp.dot(hid, wd_ref[...], preferred_element_type=F32) * gate).astype(o_ref.dtype)

    @pl.when(step == n_steps - 1)
    def _():
        wait(1 - slot)


def _moe_ffn(idx_flat, tn, w_gate, w_up, w_down, layer, slots):
    _, n_exp, d, ff = w_gate.shape
    tm = _tile(slots, FFN_ROWS_MAX, 16)
    tiles = slots // tm
    grid_spec = pltpu.PrefetchScalarGridSpec(
        num_scalar_prefetch=1,
        grid=(n_exp, tiles),
        in_specs=[pl.BlockSpec(memory_space=pl.ANY)] * 4,
        out_specs=pl.BlockSpec((tm, d), lambda e, m, idx: (e * tiles + m, 0)),
        scratch_shapes=[
            pltpu.VMEM((2, tm, tn.shape[1]), F32),
            pltpu.VMEM((d, ff), F32), pltpu.VMEM((d, ff), F32), pltpu.VMEM((ff, d), F32),
            pltpu.VMEM((d, ff), BF16), pltpu.VMEM((d, ff), BF16), pltpu.VMEM((ff, d), BF16),
            pltpu.SemaphoreType.DMA((2,)), pltpu.SemaphoreType.DMA,
        ],
    )
    idx_padded = jnp.concatenate([idx_flat, idx_flat[:tm]])
    return pl.pallas_call(
        functools.partial(_moe_ffn_kernel, tm=tm, layer=layer),
        grid_spec=grid_spec,
        out_shape=jax.ShapeDtypeStruct((n_exp * slots, d), BF16),
        compiler_params=_params("arbitrary", "arbitrary"),
        name="moe_ffn",
    )(idx_padded, tn, w_gate, w_up, w_down)


def _combine_kernel(src_ref, pieces_ref, dst_ref, chunks_ref, ye_hbm, h_ref, tgt_ref, o_ref, stage, tgt_b, sem,
                    *, static_chunks):
    i = pl.program_id(0)
    n_tiles = pl.num_programs(0)
    n_exp = tgt_ref.shape[1]
    slot = i % 2

    @pl.when(i == 0)
    def _():
        stage[...] = jnp.zeros(stage.shape, stage.dtype)

    def piece(src_row, into, dst_row):
        return pltpu.make_async_copy(ye_hbm.at[pl.ds(src_row, COMBINE_PIECE), :],
                                     stage.at[into, pl.ds(dst_row, COMBINE_PIECE), :], sem.at[into])

    def fetch(tile, into, live):
        n_pieces = 0
        for e in range(n_exp):
            src, dst = src_ref[tile * n_exp + e], dst_ref[tile * n_exp + e]
            n = pieces_ref[tile * n_exp + e] * live

            def issue(p, carry, src=src, dst=dst):
                piece(pl.multiple_of(src + p * COMBINE_PIECE, COMBINE_PIECE), into,
                      pl.multiple_of(dst + p * COMBINE_PIECE, COMBINE_PIECE)).start()
                return carry

            lax.fori_loop(0, n, issue, 0)
            n_pieces = n_pieces + n
        return n_pieces

    @pl.when(i == 0)
    def _():
        fetch(0, 0, 1)

    n_mine = 0
    for e in range(n_exp):
        n_mine = n_mine + pieces_ref[i * n_exp + e]

    def wait(p, carry):
        piece(0, slot, 0).wait()
        return carry

    lax.fori_loop(0, n_mine, wait, 0)
    nxt = jnp.minimum(i + 1, n_tiles - 1)
    fetch(nxt, 1 - slot, jnp.where(i + 1 < n_tiles, 1, 0))
    t_tile = tgt_ref.shape[0]
    for e in range(n_exp):
        tgt_b[e] = jnp.broadcast_to(tgt_ref[:, e:e + 1], (t_tile, LANES))
    lane = lax.broadcasted_iota(jnp.int32, (1, LANES), 1)

    def onehot(base, n_cols):
        parts = []
        for part in range(n_cols // LANES):
            want = lane + (base + part * LANES)
            hit = jnp.zeros((t_tile, LANES), F32)
            for e in range(n_exp):
                hit = jnp.where(tgt_b[e] == want, 1.0, hit)
            parts.append(hit.astype(BF16))
        return jnp.concatenate(parts, axis=1)

    k0 = static_chunks * COMBINE_CHUNK
    o_ref[...] = h_ref[...] + jnp.dot(onehot(0, k0), stage[slot, 0:k0, :], preferred_element_type=F32)

    def chunk(c, carry):
        base = pl.multiple_of(c * COMBINE_CHUNK, COMBINE_CHUNK)
        o_ref[...] += jnp.dot(onehot(base, COMBINE_CHUNK), stage[slot, pl.ds(base, COMBINE_CHUNK), :],
                              preferred_element_type=F32)
        return carry

    lax.fori_loop(static_chunks, chunks_ref[i], chunk, 0)


def _combine(ye, h, sel_t, row_t, t_tile):
    nt, d = h.shape
    n_exp = sel_t.shape[1]
    assert ye.shape[0] % COMBINE_PIECE == 0
    first = row_t[::t_tile]
    last = (row_t + sel_t)[t_tile - 1::t_tile]
    src = first // COMBINE_PIECE * COMBINE_PIECE
    pieces = (last - src + COMBINE_PIECE - 1) // COMBINE_PIECE
    dst = (jnp.cumsum(pieces, axis=1) - pieces) * COMBINE_PIECE
    chunks = (jnp.sum(pieces, axis=1) * COMBINE_PIECE + COMBINE_CHUNK - 1) // COMBINE_CHUNK
    tgt = jnp.where(sel_t > 0, row_t - jnp.repeat(src - dst, t_tile, axis=0), -1)
    stage_rows = pl.cdiv(n_exp * (t_tile + 2 * (COMBINE_PIECE - 1)), COMBINE_CHUNK) * COMBINE_CHUNK
    usual_rows = n_exp * (N_EXPERT_CAPACITY_FACTOR * t_tile // n_exp + COMBINE_PIECE - 1)
    static_chunks = min(pl.cdiv(usual_rows, COMBINE_CHUNK), stage_rows // COMBINE_CHUNK)
    grid_spec = pltpu.PrefetchScalarGridSpec(
        num_scalar_prefetch=4,
        grid=(nt // t_tile,),
        in_specs=[
            pl.BlockSpec(memory_space=pl.ANY),
            pl.BlockSpec((t_tile, d), lambda i, *_: (i, 0)),
            pl.BlockSpec((t_tile, n_exp), lambda i, *_: (i, 0)),
        ],
        out_specs=pl.BlockSpec((t_tile, d), lambda i, *_: (i, 0)),
        scratch_shapes=[pltpu.VMEM((2, stage_rows, d), ye.dtype), pltpu.VMEM((n_exp, t_tile, LANES), jnp.int32),
                        pltpu.SemaphoreType.DMA((2,))],
    )
    flat = lambda x: x.reshape(-1).astype(jnp.int32)
    return pl.pallas_call(
        functools.partial(_combine_kernel, static_chunks=static_chunks),
        grid_spec=grid_spec,
        out_shape=jax.ShapeDtypeStruct((nt, d), F32),
        compiler_params=_params("arbitrary"),
        name="moe_combine",
    )(flat(src), flat(pieces), flat(dst), flat(chunks), ye, h, tgt.astype(jnp.int32))


def _threshold_kernel(x_ref, lo_ref, hi_ref, *, cap):
    n_exp = x_ref.shape[0]
    cap_f = float(cap)

    def enough(e, threshold):
        return jnp.sum(jnp.where(x_ref[e] >= threshold, 1.0, 0.0), axis=(0, 1), keepdims=True) >= cap_f

    def as_f32(bits):
        return lax.bitcast_convert_type(bits, F32)

    def bisect_bits(_, bounds):
        new = []
        for e in range(n_exp):
            lo, hi = bounds[e]
            mid = lo + ((hi - lo) >> 1)
            ok = enough(e, as_f32(mid))
            new.append((jnp.where(ok, mid, lo), jnp.where(ok, hi, mid)))
        return tuple(new)

    start = (jnp.zeros((1, 1), jnp.int32), jnp.full((1, 1), F32_INF_BITS, jnp.int32))
    bounds = lax.fori_loop(0, 31, bisect_bits, tuple(start for _ in range(n_exp)))

    def bisect_values(_, bounds):
        new = []
        for e in range(n_exp):
            lo, hi = bounds[e]
            mid = lo + (hi - lo) * 0.5
            ok = enough(e, mid)
            new.append((jnp.where(ok, mid, lo), jnp.where(ok, hi, mid)))
        return tuple(new)

    bounds = lax.fori_loop(0, ROUTE_REFINE_STEPS, bisect_values, tuple((as_f32(lo), as_f32(hi)) for lo, hi in bounds))
    for e in range(n_exp):
        lo_ref[e] = jnp.broadcast_to(bounds[e][0], lo_ref.shape[1:])
        hi_ref[e] = jnp.broadcast_to(bounds[e][1], hi_ref.shape[1:])


def _route_kernel(x_ref, lo_ref, hi_ref, sel_ref, pexc_ref, idx_ref, *, cap, slot_block):
    x = x_ref[0]
    n_rows = x.shape[0]
    cap_f = float(cap)
    lo, hi = lo_ref[0, 0:1, 0:1], hi_ref[0, 0:1, 0:1]

    upper = jnp.where(lax.broadcasted_iota(jnp.int32, (LANES, LANES), 0)
                      <= lax.broadcasted_iota(jnp.int32, (LANES, LANES), 1), 1.0, 0.0).astype(BF16)
    lower = jnp.where(lax.broadcasted_iota(jnp.int32, (n_rows, n_rows), 1)
                      < lax.broadcasted_iota(jnp.int32, (n_rows, n_rows), 0), 1.0, 0.0).astype(BF16)

    def prefix(m):
        within = jnp.dot(m.astype(BF16), upper, preferred_element_type=F32)
        row_tot = jnp.broadcast_to(within[:, LANES - 1:], within.shape)
        before = jnp.dot(lower, row_tot.astype(BF16), preferred_element_type=F32)
        return within, before, row_tot

    gt = jnp.where(x >= hi, 1.0, 0.0)
    eq = jnp.where(x >= lo, 1.0, 0.0) - gt
    need = cap_f - jnp.sum(gt, axis=(0, 1), keepdims=True)
    tie_within, tie_before, _ = prefix(eq)
    tie_rank = tie_within + tie_before - eq
    sel = gt + eq * jnp.where(tie_rank < need, 1.0, 0.0)
    within, before, row_tot = prefix(sel)
    sel_ref[0] = sel.astype(jnp.int32)
    pexc_ref[0] = (within + before - sel).astype(jnp.int32)

    row_end = jnp.transpose(before + row_tot)[0:1, :]
    row_start = jnp.transpose(before)[0:1, :]
    within_b = within.astype(BF16)
    row_id = lax.broadcasted_iota(jnp.int32, (1, n_rows), 1).astype(F32)
    for s in range(idx_ref.shape[1] // slot_block):
        j = (lax.broadcasted_iota(jnp.int32, (slot_block, 1), 0) + s * slot_block).astype(F32)
        row = jnp.sum(jnp.where(row_end <= j, 1.0, 0.0), axis=-1, keepdims=True)
        onehot = jnp.where(row_id == row, 1.0, 0.0)
        cum = jnp.dot(onehot.astype(BF16), within_b, preferred_element_type=F32)
        local = j - jnp.sum(onehot * row_start, axis=-1, keepdims=True)
        col = jnp.sum(jnp.where(cum <= local, 1.0, 0.0), axis=-1, keepdims=True)
        tok = (row * LANES + col).astype(jnp.int32)
        idx_ref[0, s * slot_block:(s + 1) * slot_block, :] = jnp.where(j < cap_f, tok, 0)


def _route(aff_rows, cap):
    n_exp, n_rows, _ = aff_rows.shape
    cap_pad = pl.cdiv(cap, ROUTE_SLOT_BLOCK) * ROUTE_SLOT_BLOCK
    blk = pl.BlockSpec((1, n_rows, LANES), lambda e: (e, 0, 0))
    bound = jax.ShapeDtypeStruct((n_exp, 8, LANES), F32)
    bound_blk = pl.BlockSpec((1, 8, LANES), lambda e: (e, 0, 0))
    lo, hi = pl.pallas_call(
        functools.partial(_threshold_kernel, cap=cap),
        out_shape=[bound, bound],
        compiler_params=pltpu.CompilerParams(vmem_limit_bytes=VMEM_LIMIT_BYTES),
        name="route_threshold",
    )(aff_rows)
    return pl.pallas_call(
        functools.partial(_route_kernel, cap=cap, slot_block=ROUTE_SLOT_BLOCK),
        grid=(n_exp,),
        in_specs=[blk, bound_blk, bound_blk],
        out_specs=[blk, blk, pl.BlockSpec((1, cap_pad, 1), lambda e: (e, 0, 0))],
        out_shape=[
            jax.ShapeDtypeStruct(aff_rows.shape, jnp.int32),
            jax.ShapeDtypeStruct(aff_rows.shape, jnp.int32),
            jax.ShapeDtypeStruct((n_exp, cap_pad, 1), jnp.int32),
        ],
        compiler_params=_params("parallel"),
        name="route",
    )(aff_rows, lo, hi)


def _moe(h, tn, aff, group_sizes, w_gate, w_up, w_down, layer):
    n_exp = aff.shape[1]
    caps = [N_EXPERT_CAPACITY_FACTOR * n // n_exp for n in group_sizes]
    slots = pl.cdiv(sum(caps), FFN_SLOT_ALIGN) * FFN_SLOT_ALIGN
    idx_parts, sel_parts, row_parts = [], [], []
    tok0, slot0 = 0, 0
    for n, cap in zip(group_sizes, caps):
        n_rows = pl.cdiv(pl.cdiv(n, LANES), LANES) * LANES
        rows = jnp.pad(aff[tok0: tok0 + n].T, ((0, 0), (0, n_rows * LANES - n)), constant_values=-1.0)
        sel, pexc, idx = _route(rows.reshape(n_exp, n_rows, LANES), cap)
        idx_parts.append(idx[:, :cap, 0] + tok0)
        sel_parts.append(sel.reshape(n_exp, -1)[:, :n].T)
        expert_row0 = jnp.arange(n_exp, dtype=jnp.int32)[None, :] * slots + slot0
        row_parts.append(pexc.reshape(n_exp, -1)[:, :n].T + expert_row0)
        tok0 += n
        slot0 += cap
    idx_parts.append(jnp.zeros((n_exp, slots - sum(caps)), jnp.int32))
    idx_all = jnp.concatenate(idx_parts, axis=1)
    sel_t = jnp.concatenate(sel_parts, axis=0)
    row_t = jnp.concatenate(row_parts, axis=0)
    ye = _moe_ffn(idx_all.reshape(-1), tn, w_gate, w_up, w_down, layer, slots)
    t_tile = _tile(math.gcd(*group_sizes), COMBINE_TOKENS_MAX, 8)
    return _combine(ye, h, sel_t, row_t, t_tile)


def _final_norm_kernel(x_ref, g_ref, o_ref):
    o_ref[...] = _rms(x_ref[...], g_ref[...])


def _final_norm(h, g, first_seq, n_seqs, seq):
    _, d = h.shape
    s_out = seq - N_META
    tm = _tile(s_out, 512, 8)
    per_seq = s_out // tm
    out = pl.pallas_call(
        _final_norm_kernel,
        grid=(n_seqs, per_seq),
        in_specs=[
            pl.BlockSpec((pl.Element(tm), pl.Element(d)),
                         lambda b, j: (pl.multiple_of((first_seq + b) * seq + N_META + j * tm, 8), 0)),
            pl.BlockSpec((1, d), lambda b, j: (0, 0)),
        ],
        out_specs=pl.BlockSpec((tm, d), lambda b, j: (b * per_seq + j, 0)),
        out_shape=jax.ShapeDtypeStruct((n_seqs * s_out, d), F32),
        compiler_params=_params("parallel", "parallel"),
        name="final_norm",
    )(h, g)
    return out.reshape(n_seqs, s_out, d)


def _lambda_init(layer):
    return 0.8 - 0.6 * math.exp(-0.3 * layer)


def kernel(x_prompt, x_sample, meta_tokens, mix_norm_g, ffn_norm_g, final_norm_g, ev_w_in, ev_w_out, ev_lambda_q1, ev_lambda_k1, ev_lambda_q2, ev_lambda_k2, ev_subln_g, ev_conv_w, od_w_in, od_q_norm_g, od_w_q_b, od_kv_norm_g, od_w_kv_b, od_w_out, moe_w_router, moe_w_gate, moe_w_up, moe_w_down):
    d = x_prompt.shape[-1]
    seq = N_META + x_prompt.shape[1]
    assert x_sample.shape[1] == x_prompt.shape[1]
    nb_p, nb_s = x_prompt.shape[0], x_sample.shape[0]
    nb = nb_p + nb_s
    nt = nb * seq
    group_sizes = (nb_p * seq, nb_s * seq)
    depth = mix_norm_g.shape[0]

    x = jnp.concatenate([x_prompt, x_sample], axis=0)
    meta = jnp.broadcast_to(meta_tokens[None].astype(x.dtype), (nb, N_META, d))
    h = jnp.concatenate([meta, x], axis=1).reshape(nt, d)
    cos128, sin128 = _rope_tables(seq)

    for layer in range(depth):
        i = layer // 2
        mix_g = mix_norm_g[layer][None]
        ffn_g = ffn_norm_g[layer][None]
        w_router = moe_w_router[layer].astype(BF16)
        if layer % 2 == 0:
            width = d // 2
            heads = width // LANES
            w_in = ev_w_in[i].astype(BF16)
            qk3 = _even_proj(h, mix_g, w_in, cos128, sin128, seq, True).reshape(nb, seq, 2 * width)
            rest3 = _even_proj(h, mix_g, w_in, cos128, sin128, seq, False).reshape(nb, seq, 4 * width)
            attn = _diff_attn(qk3, rest3, ev_lambda_q1[i][None], ev_lambda_k1[i][None], ev_lambda_q2[i][None],
                              ev_lambda_k2[i][None], ev_subln_g[i][None], _lambda_init(layer), heads)
            conv = _gated_conv(rest3, ev_conv_w[i], width)
            h, tn, aff = _out_proj_router(attn.reshape(nt, width), conv.reshape(nt, width), 0, 0,
                                          ev_w_out[i].astype(BF16), h, ffn_g, w_router)
        else:
            q_rank, kv_rank = od_q_norm_g.shape[1], od_kv_norm_g.shape[1]
            heads = od_w_q_b.shape[2] // (MLA_NOPE + MLA_ROPE)
            w_in_pad = jnp.pad(od_w_in[i], ((0, 0), (0, LANES - MLA_ROPE))).astype(BF16)
            w_q = od_w_q_b[i].reshape(q_rank, heads, MLA_NOPE + MLA_ROPE)
            w_q = jnp.pad(w_q, ((0, 0), (0, 0), (0, MLA_QK_PAD - MLA_NOPE - MLA_ROPE)))
            w_q = w_q.reshape(q_rank, heads * MLA_QK_PAD).astype(BF16)
            w_kv = od_w_kv_b[i].reshape(kv_rank, heads, MLA_NOPE + MLA_V)
            w_kv = jnp.concatenate([w_kv[:, :, :MLA_NOPE].reshape(kv_rank, heads * MLA_NOPE),
                                    w_kv[:, :, MLA_NOPE:].reshape(kv_rank, heads * MLA_V)], axis=1).astype(BF16)
            q, kv, kr = _odd_proj(h, mix_g, w_in_pad, od_q_norm_g[i][None], od_kv_norm_g[i][None], w_q, w_kv,
                                  cos128, sin128, seq, (MLA_NOPE + MLA_ROPE) ** -0.5 * LOG2E)
            o = _mla_attn(q.reshape(nb, seq, heads * MLA_QK_PAD), kv.reshape(nb, seq, 2 * heads * LANES),
                          kr.reshape(nb, seq, LANES), heads)
            o = o.reshape(nt, heads * MLA_V)
            h, tn, aff = _out_proj_router(o, o, 0, 1, od_w_out[i].astype(BF16), h, ffn_g, w_router)
        h = _moe(h, tn, aff, group_sizes, moe_w_gate, moe_w_up, moe_w_down, layer)

    g = final_norm_g[None]
    return (_final_norm(h, g, 0, nb_p, seq), _final_norm(h, g, nb_p, nb_s, seq))
```
